```python
import math
import jax, jax.numpy as jnp
from jax import lax
import numpy as np

D_MODEL = 4096
BATCH = 2
SEQ = 4096
DEPTH = 2

EPS = 1e-6
M_HEADS = 8
M_DQK = 256
M_DV = 512
M_QK = M_HEADS * M_DQK
M_V = M_HEADS * M_DV
M_CHUNK = 64
M_CONV = 3
M_NGATE = 2 * 2 * M_HEADS
M_COLS = 2 * M_QK + 2 * M_V + M_NGATE
H_WIDTH = 4096
H_ORDER = 2
H_CONV = 3
H_BANDS = 16
H_EMB = 1 + 2 * H_BANDS
H_HID = 64
H_FAST = 0.3
H_SLOW = 1.5
H_TARGET = 1e-2
H_COLS = (H_ORDER + 1) * H_WIDTH
G_COLS = 2 * D_MODEL
P_TOTAL = M_COLS + H_COLS + G_COLS
P_HEADS = 8
N_KEYS = 128
N_EXP = N_KEYS * N_KEYS
P_DKEY = 128
P_TOPK = 16
P_BLOCK = 64

kernel_name = "hybrid_mlstm_hyena_peer_encoder"


def rms_norm(x, g):
    xf = x.astype(jnp.float32)
    y = xf * lax.rsqrt(jnp.mean(xf * xf, axis=-1, keepdims=True) + EPS)
    return (y * g.astype(jnp.float32)).astype(x.dtype)


def centred_dwconv(x, w):
    K, C = w.shape
    p = K // 2
    return lax.conv_general_dilated(x, w.astype(x.dtype)[:, None, :], window_strides=(1,), padding=((p, p),), dimension_numbers=('NWC', 'WIO', 'NWC'), feature_group_count=C)


def mlstm_scan(q, k, v, li, lf):
    B, H, S, dk = q.shape
    dv = v.shape[-1]
    L = M_CHUNK
    nc = S // L

    def to_chunks(a):
        return jnp.moveaxis(a.reshape(B, H, nc, L, *a.shape[3:]), 2, 0)

    xs = (to_chunks(q), to_chunks(k), to_chunks(v), to_chunks(li), to_chunks(lf))
    lower = jnp.tril(jnp.ones((L, L), dtype=bool))

    def step(carry, blk):
        C, n, m = carry
        qb, kb, vb, ib, fb = blk
        b = jnp.cumsum(fb, axis=-1)
        dlog = b[..., :, None] - b[..., None, :] + ib[..., None, :]
        dlog = jnp.where(lower, dlog, -jnp.inf)
        inter = b + m[..., None]
        m_t = jnp.maximum(inter, jnp.max(dlog, axis=-1))
        w_intra = jnp.exp(dlog - m_t[..., None])
        w_inter = jnp.exp(inter - m_t)
        s = jnp.einsum('bhld,bhsd->bhls', qb, kb) * w_intra
        num = w_inter[..., None] * jnp.einsum('bhld,bhde->bhle', qb, C) + jnp.einsum('bhls,bhse->bhle', s, vb)
        den = w_inter * jnp.einsum('bhld,bhd->bhl', qb, n) + jnp.sum(s, axis=-1)
        h = num / jnp.maximum(jnp.abs(den), jnp.exp(-m_t))[..., None]
        bL = b[..., -1]
        gl = bL[..., None] - b + ib
        m_new = jnp.maximum(bL + m, jnp.max(gl, axis=-1))
        dec = jnp.exp(bL + m - m_new)
        wk = jnp.exp(gl - m_new[..., None])
        C_new = dec[..., None, None] * C + jnp.einsum('bhs,bhsd,bhse->bhde', wk, kb, vb)
        n_new = dec[..., None] * n + jnp.einsum('bhs,bhsd->bhd', wk, kb)
        return (C_new, n_new, m_new), h

    init = (jnp.zeros((B, H, dk, dv), jnp.float32), jnp.zeros((B, H, dk), jnp.float32), jnp.zeros((B, H), jnp.float32))
    _, hs = lax.scan(step, init, xs)
    return jnp.moveaxis(hs, 0, 2).reshape(B, H, S, dv)


def mlstm_branch(u, conv_w, gate_b, norm_g):
    f32 = jnp.float32
    B, S, _ = u.shape
    qk = jax.nn.silu(centred_dwconv(u[..., :2 * M_QK], conv_w)).astype(f32)
    q = qk[..., :M_QK].reshape(B, S, M_HEADS, M_DQK).transpose(0, 2, 1, 3)
    k = qk[..., M_QK:].reshape(B, S, M_HEADS, M_DQK).transpose(0, 2, 1, 3) * (M_DQK ** -0.5)
    v = u[..., 2 * M_QK:2 * M_QK + M_V].astype(f32).reshape(B, S, M_HEADS, M_DV).transpose(0, 2, 1, 3)
    o = u[..., 2 * M_QK + M_V:2 * M_QK + 2 * M_V].astype(f32)
    gates = (u[..., 2 * M_QK + 2 * M_V:].astype(f32) + gate_b.astype(f32)).reshape(B, S, 2, 2, M_HEADS)
    gates = jnp.transpose(gates, (2, 3, 0, 4, 1))
    h_f = mlstm_scan(q, k, v, gates[0, 0], jax.nn.log_sigmoid(gates[0, 1]))
    flip = lambda a: jnp.flip(a, axis=2)
    h_b = flip(mlstm_scan(flip(q), flip(k), flip(v), flip(gates[1, 0]), flip(jax.nn.log_sigmoid(gates[1, 1]))))
    h = h_f + h_b
    h = h * lax.rsqrt(jnp.mean(h * h, axis=-1, keepdims=True) + EPS)
    h = h.transpose(0, 2, 1, 3).reshape(B, S, M_V) * norm_g.astype(f32)
    return h * jax.nn.sigmoid(o)


def hyena_filter_spectrum(L, w1, b1, w2, b2, w3, b3, freq, decay):
    f32 = jnp.float32
    tn = jnp.arange(L, dtype=f32) / L
    bands = jnp.linspace(1e-4, H_BANDS - 1, H_BANDS, dtype=f32)
    ang = (2.0 * math.pi) * tn[:, None] * bands[None, :]
    z = jnp.concatenate([tn[:, None], jnp.cos(ang), jnp.sin(ang)], axis=-1)
    fr = freq.astype(f32)
    h = jnp.sin(fr * (z @ w1.astype(f32) + b1.astype(f32)))
    h = jnp.sin(fr * (h @ w2.astype(f32) + b2.astype(f32)))
    h = (h @ w3.astype(f32) + b3.astype(f32)).reshape(L, 2, H_ORDER, H_WIDTH)
    h = h * jnp.exp(-tn[:, None, None, None] * jnp.abs(decay.astype(f32))[None])
    hf, hb = h[:, 0], h[:, 1]
    k2 = jnp.concatenate([hf, jnp.zeros_like(hf[:1]), hb[:0:-1]], axis=0)
    k2 = k2 / jnp.sum(jnp.abs(k2), axis=0, keepdims=True)
    return jnp.fft.rfft(k2, n=2 * L, axis=0)


def long_conv(z, kf):
    L = z.shape[1]
    zf = jnp.fft.rfft(z, n=2 * L, axis=1)
    return jnp.fft.irfft(zf * kf[None], n=2 * L, axis=1)[:, :L]


def hyena_branch(u, conv_w, kf, skip):
    f32 = jnp.float32
    zc = centred_dwconv(u, conv_w).astype(f32)
    v, x1, x2 = jnp.split(zc, 3, axis=-1)
    z = v
    for o, gate in enumerate((x1, x2)):
        z = gate * (long_conv(z, kf[:, o]) + skip[o].astype(f32) * z)
    return z


def peer(xn, wq, keys, U, V):
    f32 = jnp.float32
    B, S, D = xn.shape
    T = B * S
    xt = xn.reshape(T, D)
    q = (xt @ wq).astype(f32).reshape(T, P_HEADS, 2, P_DKEY)
    s = jnp.einsum('thpd,hpkd->thpk', q, keys.astype(f32))
    sv, si = lax.top_k(s, P_TOPK)
    cand = (sv[:, :, 0, :, None] + sv[:, :, 1, None, :]).reshape(T, P_HEADS, P_TOPK * P_TOPK)
    cidx = (si[:, :, 0, :, None] * N_KEYS + si[:, :, 1, None, :]).reshape(T, P_HEADS, P_TOPK * P_TOPK)
    top_s, pos = lax.top_k(cand, P_TOPK)
    eidx = jnp.take_along_axis(cidx, pos, axis=-1)
    gate = jax.nn.softmax(top_s, axis=-1)
    nb = T // P_BLOCK

    def blk(args):
        xb, ib, gb = args
        a = jnp.einsum('thkd,td->thk', jnp.take(U, ib, axis=0), xb).astype(f32)
        w = (gb * jax.nn.gelu(a, approximate=False)).astype(xb.dtype)
        return jnp.einsum('thk,thkd->td', w, jnp.take(V, ib, axis=0))

    y = lax.map(blk, (xt.reshape(nb, P_BLOCK, D), eidx.reshape(nb, P_BLOCK, P_HEADS, P_TOPK), gate.reshape(nb, P_BLOCK, P_HEADS, P_TOPK)))
    return y.reshape(B, S, D).astype(xn.dtype)


def setup_inputs(seed: int = 0) -> dict:
    key = jax.random.key(seed)
    ks = jax.random.split(key, 32)
    f32 = jnp.float32

    def nrm(k, shape, scale):
        return jax.random.normal(k, shape, f32) * scale

    x = nrm(ks[0], (BATCH, SEQ, D_MODEL), 1.0)
    ln1_g = 1.0 + nrm(ks[1], (DEPTH, D_MODEL), 0.02)
    ln2_g = 1.0 + nrm(ks[2], (DEPTH, D_MODEL), 0.02)
    w_in = nrm(ks[3], (DEPTH, D_MODEL, P_TOTAL), D_MODEL ** -0.5)
    m_conv = nrm(ks[4], (DEPTH, M_CONV, 2 * M_QK), M_CONV ** -0.5)
    i_b = nrm(ks[5], (DEPTH, 2, 1, M_HEADS), 0.1)
    f_b = jnp.broadcast_to(jnp.linspace(3.0, 6.0, M_HEADS, dtype=f32), (DEPTH, 2, 1, M_HEADS)) + nrm(ks[6], (DEPTH, 2, 1, M_HEADS), 0.1)
    m_gate_b = jnp.concatenate([i_b, f_b], axis=2).reshape(DEPTH, M_NGATE)
    m_norm_g = 1.0 + nrm(ks[7], (DEPTH, M_V), 0.02)
    hy_conv = nrm(ks[8], (DEPTH, H_CONV, H_COLS), H_CONV ** -0.5)
    hy_w1 = nrm(ks[9], (DEPTH, H_EMB, H_HID), H_EMB ** -0.5)
    hy_b1 = nrm(ks[10], (DEPTH, H_HID), 0.02)
    hy_w2 = nrm(ks[11], (DEPTH, H_HID, H_HID), H_HID ** -0.5)
    hy_b2 = nrm(ks[12], (DEPTH, H_HID), 0.02)
    hy_w3 = nrm(ks[13], (DEPTH, H_HID, 2 * H_ORDER * H_WIDTH), H_HID ** -0.5)
    hy_b3 = nrm(ks[14], (DEPTH, 2 * H_ORDER * H_WIDTH), 0.02)
    hy_freq = 1.0 + nrm(ks[15], (DEPTH, H_HID), 0.02)
    base_decay = jnp.linspace(math.log(H_TARGET) / H_SLOW, math.log(H_TARGET) / H_FAST, H_WIDTH, dtype=f32)
    hy_decay = base_decay * (1.0 + nrm(ks[16], (DEPTH, 2, H_ORDER, H_WIDTH), 0.05))
    hy_skip = nrm(ks[17], (DEPTH, H_ORDER, H_WIDTH), 1.0)
    w_mo = nrm(ks[18], (DEPTH, M_V, D_MODEL), M_V ** -0.5)
    w_ho = nrm(ks[19], (DEPTH, H_WIDTH, D_MODEL), H_WIDTH ** -0.5)
    w_o = nrm(ks[20], (DEPTH, D_MODEL, D_MODEL), D_MODEL ** -0.5)
    peer_wq = nrm(ks[21], (DEPTH, D_MODEL, P_HEADS * 2 * P_DKEY), D_MODEL ** -0.5)
    peer_keys = nrm(ks[22], (DEPTH, P_HEADS, 2, N_KEYS, P_DKEY), P_DKEY ** -0.5)
    peer_u = nrm(ks[23], (DEPTH, N_EXP, D_MODEL), D_MODEL ** -0.5)
    peer_v = nrm(ks[24], (DEPTH, N_EXP, D_MODEL), P_HEADS ** -0.5)
    final_g = 1.0 + nrm(ks[25], (D_MODEL,), 0.02)
    return {"x": x, "ln1_g": ln1_g, "ln2_g": ln2_g, "w_in": w_in, "m_conv": m_conv, "m_gate_b": m_gate_b, "m_norm_g": m_norm_g, "hy_conv": hy_conv, "hy_w1": hy_w1, "hy_b1": hy_b1, "hy_w2": hy_w2, "hy_b2": hy_b2, "hy_w3": hy_w3, "hy_b3": hy_b3, "hy_freq": hy_freq, "hy_decay": hy_decay, "hy_skip": hy_skip, "w_mo": w_mo, "w_ho": w_ho, "w_o": w_o, "peer_wq": peer_wq, "peer_keys": peer_keys, "peer_u": peer_u, "peer_v": peer_v, "final_g": final_g}


def reference(x, ln1_g, ln2_g, w_in, m_conv, m_gate_b, m_norm_g, hy_conv, hy_w1, hy_b1, hy_w2, hy_b2, hy_w3, hy_b3, hy_freq, hy_decay, hy_skip, w_mo, w_ho, w_o, peer_wq, peer_keys, peer_u, peer_v, final_g):
    f32 = jnp.float32
    B, S, D = x.shape
    for l in range(DEPTH):
        xn = rms_norm(x, ln1_g[l])
        u = xn @ w_in[l]
        u_m = u[..., :M_COLS]
        u_h = u[..., M_COLS:M_COLS + H_COLS]
        u_g = u[..., M_COLS + H_COLS:]
        a = mlstm_branch(u_m, m_conv[l], m_gate_b[l], m_norm_g[l]).astype(x.dtype) @ w_mo[l]
        kf = hyena_filter_spectrum(S, hy_w1[l], hy_b1[l], hy_w2[l], hy_b2[l], hy_w3[l], hy_b3[l], hy_freq[l], hy_decay[l])
        b = hyena_branch(u_h, hy_conv[l], kf, hy_skip[l]).astype(x.dtype) @ w_ho[l]
        g = jax.nn.sigmoid(u_g.astype(f32)).reshape(B, S, 2, D)
        merged = (g[:, :, 0] * a.astype(f32) + g[:, :, 1] * b.astype(f32)).astype(x.dtype)
        x = x + merged @ w_o[l]
        x = x + peer(rms_norm(x, ln2_g[l]), peer_wq[l], peer_keys[l], peer_u[l], peer_v[l])
    return rms_norm(x, final_g)
```

```python
import functools
import math

import jax
import jax.numpy as jnp
from jax import lax
from jax.experimental import pallas as pl
from jax.experimental.pallas import tpu as pltpu

F32 = jnp.float32
BF16 = jnp.bfloat16

EPS = 1e-6
M_HEADS = 8
M_DQK = 256
M_DV = 512
M_CHUNK = 256
H_ORDER = 2
H_BANDS = 16
P_HEADS = 8
N_KEYS = 128
P_DKEY = 128
P_TOPK = 16

V7X_VMEM_BYTES = 64 * 1024 * 1024
VMEM_LIMIT = V7X_VMEM_BYTES - 8 * 1024 * 1024
LANES = 128
HI = lax.Precision.HIGHEST


def _cparams(*sem):
    return pltpu.CompilerParams(dimension_semantics=sem, vmem_limit_bytes=VMEM_LIMIT)


def _tile(dim, pref):
    t = min(dim, pref)
    while dim % t:
        t -= LANES
    assert t > 0 and dim % t == 0, (dim, pref)
    return t


def _rmsnorm_body(x_ref, g_ref, o_ref):
    x = x_ref[...]
    ms = jnp.mean(x * x, axis=-1, keepdims=True)
    o_ref[...] = (x * lax.rsqrt(ms + EPS) * g_ref[...]).astype(o_ref.dtype)


def _rmsnorm(x, g, out_dtype):
    T, D = x.shape
    tm = _tile(T, 256)
    return pl.pallas_call(
        _rmsnorm_body,
        grid=(T // tm,),
        in_specs=[pl.BlockSpec((tm, D), lambda i: (i, 0)),
                  pl.BlockSpec((1, D), lambda i: (0, 0))],
        out_specs=pl.BlockSpec((tm, D), lambda i: (i, 0)),
        out_shape=jax.ShapeDtypeStruct((T, D), out_dtype),
        compiler_params=_cparams("parallel"),
        name="rmsnorm",
    )(x, g.reshape(1, D).astype(F32))


def _mm_body(a_ref, b_ref, *rest, has_bias, has_res):
    o_ref = rest[-1]
    acc = jnp.dot(a_ref[...], b_ref[...], preferred_element_type=F32)
    if has_bias:
        acc = acc + rest[0][...]
    if has_res:
        acc = acc + rest[1 if has_bias else 0][...]
    o_ref[...] = acc.astype(o_ref.dtype)


def _matmul(a, b, out_dtype, bias=None, residual=None, tm=512, tn=1024, name="matmul"):
    M, K = a.shape
    _, N = b.shape
    tm, tn = _tile(M, tm), _tile(N, tn)
    m_outer = a.size * (N // tn) + b.size >= a.size + b.size * (M // tm)
    if m_outer:
        grid = (M // tm, N // tn)
        im = lambda i, j: (i, 0)
        jn = lambda i, j: (0, j)
        ij = lambda i, j: (i, j)
    else:
        grid = (N // tn, M // tm)
        im = lambda j, i: (i, 0)
        jn = lambda j, i: (0, j)
        ij = lambda j, i: (i, j)
    in_specs = [pl.BlockSpec((tm, K), im), pl.BlockSpec((K, tn), jn)]
    args = [a, b]
    if bias is not None:
        in_specs.append(pl.BlockSpec((1, tn), jn))
        args.append(bias.reshape(1, N).astype(F32))
    if residual is not None:
        in_specs.append(pl.BlockSpec((tm, tn), ij))
        args.append(residual)
    return pl.pallas_call(
        functools.partial(_mm_body, has_bias=bias is not None, has_res=residual is not None),
        grid=grid,
        in_specs=in_specs,
        out_specs=pl.BlockSpec((tm, tn), ij),
        out_shape=jax.ShapeDtypeStruct((M, N), out_dtype),
        compiler_params=_cparams("parallel", "parallel"),
        name=name,
    )(*args)


def _merge_body(a1_ref, b1_ref, a2_ref, b2_ref, g1_ref, g2_ref, o_ref):
    y1 = jnp.dot(a1_ref[...], b1_ref[...], preferred_element_type=F32)
    y2 = jnp.dot(a2_ref[...], b2_ref[...], preferred_element_type=F32)
    g1 = jax.nn.sigmoid(g1_ref[...].astype(F32))
    g2 = jax.nn.sigmoid(g2_ref[...].astype(F32))
    o_ref[...] = (g1 * y1 + g2 * y2).astype(o_ref.dtype)


def _merge_matmul(a1, b1, a2, b2, u, g1_off, g2_off, tm=512, tn=512):
    M, K = a1.shape
    _, N = b1.shape
    tm, tn = _tile(M, tm), _tile(N, tn)
    o1, o2 = g1_off // tn, g2_off // tn
    assert g1_off % tn == 0 and g2_off % tn == 0
    return pl.pallas_call(
        _merge_body,
        grid=(M // tm, N // tn),
        in_specs=[pl.BlockSpec((tm, K), lambda i, j: (i, 0)),
                  pl.BlockSpec((K, tn), lambda i, j: (0, j)),
                  pl.BlockSpec((tm, K), lambda i, j: (i, 0)),
                  pl.BlockSpec((K, tn), lambda i, j: (0, j)),
                  pl.BlockSpec((tm, tn), lambda i, j: (i, o1 + j)),
                  pl.BlockSpec((tm, tn), lambda i, j: (i, o2 + j))],
        out_specs=pl.BlockSpec((tm, tn), lambda i, j: (i, j)),
        out_shape=jax.ShapeDtypeStruct((M, N), BF16),
        compiler_params=_cparams("parallel", "parallel"),
        name="merge_matmul",
    )(a1, b1, a2, b2, u, u)


def _dwconv_body(x_ref, w_ref, s_ref, o_ref, *, silu):
    x = x_ref[0].astype(F32)
    S = x.shape[0]
    row = lax.broadcasted_iota(jnp.int32, x.shape, 0)
    xm = jnp.where(row == 0, 0.0, pltpu.roll(x, 1, axis=0))
    xp = jnp.where(row == S - 1, 0.0, pltpu.roll(x, S - 1, axis=0))
    w = w_ref[...]
    y = w[0:1, :] * xm + w[1:2, :] * x + w[2:3, :] * xp
    if silu:
        y = y * jax.nn.sigmoid(y)
    o_ref[0] = (y * s_ref[...]).astype(o_ref.dtype)


def _dwconv(u3, col_off, w, scale, silu, tc=256):
    B, S, _ = u3.shape
    C = w.shape[1]
    tc = _tile(C, tc)
    assert col_off % tc == 0
    o = col_off // tc
    return pl.pallas_call(
        functools.partial(_dwconv_body, silu=silu),
        grid=(B, C // tc),
        in_specs=[pl.BlockSpec((1, S, tc), lambda b, j: (b, 0, o + j)),
                  pl.BlockSpec((3, tc), lambda b, j: (0, j)),
                  pl.BlockSpec((1, tc), lambda b, j: (0, j))],
        out_specs=pl.BlockSpec((1, S, tc), lambda b, j: (b, 0, j)),
        out_shape=jax.ShapeDtypeStruct((B, S, C), BF16),
        compiler_params=_cparams("parallel", "parallel"),
        name="dwconv_silu" if silu else "dwconv",
    )(u3, w.astype(F32), scale.reshape(1, C).astype(F32))


def _log_sigmoid(x):
    return jnp.minimum(x, 0.0) - jnp.log1p(jnp.exp(-jnp.abs(x)))


def _mlstm_body(q_ref, k_ref, v_ref, o_ref, grow_ref, gcol_ref, ng_ref, out_ref,
                c_ref, n_ref, m_ref, h_ref, *, L):
    S = q_ref.shape[1]
    nc = S // L
    row = lax.broadcasted_iota(jnp.int32, (L, L), 0)
    col = lax.broadcasted_iota(jnp.int32, (L, L), 1)

    def direction(rev):
        gi, gf = (2, 3) if rev else (0, 1)
        keep = (col >= row) if rev else (col <= row)
        cum_rs = ((row >= col) if rev else (row <= col)).astype(F32)
        cum_tr = keep.astype(F32)
        last = 0 if rev else L - 1

        c_ref[...] = jnp.zeros_like(c_ref)
        n_ref[...] = jnp.zeros_like(n_ref)
        m_ref[...] = jnp.zeros_like(m_ref)

        def chunk(c, carry):
            cc = (nc - 1 - c) if rev else c
            r0 = pl.multiple_of(cc * L, L)
            q = q_ref[0, pl.ds(r0, L), :]
            k = k_ref[0, pl.ds(r0, L), :]
            v = v_ref[0, pl.ds(r0, L), :]
            grow = grow_ref[0, 0, :, pl.ds(r0, L)]
            gcol = gcol_ref[0, 0, pl.ds(r0, L), :]
            b_rows = jnp.dot(_log_sigmoid(grow), cum_rs, precision=HI, preferred_element_type=F32)
            b_cols = jnp.dot(cum_tr, _log_sigmoid(gcol), precision=HI, preferred_element_type=F32)
            b_r = b_rows[gf:gf + 1, :]
            b_c = b_cols[:, gf:gf + 1]
            li_r = grow[gi:gi + 1, :]
            li_c = gcol[:, gi:gi + 1]
            m_prev = m_ref[...]

            dlog = jnp.where(keep, b_c - b_r + li_r, -jnp.inf)
            inter = b_c + m_prev
            m_t = jnp.maximum(inter, jnp.max(dlog, axis=1, keepdims=True))
            w_intra = jnp.exp(dlog - m_t)
            w_inter = jnp.exp(inter - m_t)
            s = lax.dot_general(q, k, (((1,), (1,)), ((), ())), preferred_element_type=F32) * w_intra
            cmat = c_ref[...]
            num = w_inter * jnp.dot(q, cmat.astype(BF16), preferred_element_type=F32) \
                + jnp.dot(s.astype(BF16), v, preferred_element_type=F32)
            qn = jnp.sum(q.astype(F32) * n_ref[...], axis=1, keepdims=True)
            den = w_inter * qn + jnp.sum(s, axis=1, keepdims=True)
            h = num / jnp.maximum(jnp.abs(den), jnp.exp(-m_t))

            b_last = b_r[:, last:last + 1]
            gl = b_last - b_c + li_c
            m_new = jnp.maximum(b_last + m_prev, jnp.max(gl, axis=0, keepdims=True))
            dec = jnp.exp(b_last + m_prev - m_new)
            kw = k.astype(F32) * jnp.exp(gl - m_new)
            c_ref[...] = dec * cmat + lax.dot_general(
                kw.astype(BF16), v, (((0,), (0,)), ((), ())), preferred_element_type=F32)
            n_ref[...] = dec * n_ref[...] + jnp.sum(kw, axis=0, keepdims=True)
            m_ref[...] = m_new

            if not rev:
                h_ref[pl.ds(r0, L), :] = h
            else:
                hs = h_ref[pl.ds(r0, L), :] + h
                hn = hs * lax.rsqrt(jnp.mean(hs * hs, axis=1, keepdims=True) + EPS)
                gate = jax.nn.sigmoid(o_ref[0, pl.ds(r0, L), :].astype(F32))
                out_ref[0, pl.ds(r0, L), :] = (hn * ng_ref[...] * gate).astype(out_ref.dtype)
            return carry

        lax.fori_loop(0, nc, chunk, 0)

    direction(False)
    direction(True)


def _mlstm(qk, u3, v_off, o_off, grow, gcol, norm_g):
    B, S, _ = qk.shape
    H, dk, dv = M_HEADS, M_DQK, M_DV
    L = _tile(S, M_CHUNK)
    vo, oo = v_off // dv, o_off // dv
    assert v_off % dv == 0 and o_off % dv == 0
    return pl.pallas_call(
        functools.partial(_mlstm_body, L=L),
        grid=(B, H),
        in_specs=[pl.BlockSpec((1, S, dk), lambda b, h: (b, 0, h)),
                  pl.BlockSpec((1, S, dk), lambda b, h: (b, 0, H + h)),
                  pl.BlockSpec((1, S, dv), lambda b, h: (b, 0, vo + h)),
                  pl.BlockSpec((1, S, dv), lambda b, h: (b, 0, oo + h)),
                  pl.BlockSpec((1, 1, 8, S), lambda b, h: (b, h, 0, 0)),
                  pl.BlockSpec((1, 1, S, 8), lambda b, h: (b, h, 0, 0)),
                  pl.BlockSpec((1, dv), lambda b, h: (0, h))],
        out_specs=pl.BlockSpec((1, S, dv), lambda b, h: (b, 0, h)),
        out_shape=jax.ShapeDtypeStruct((B, S, H * dv), BF16),
        scratch_shapes=[pltpu.VMEM((dk, dv), F32), pltpu.VMEM((1, dk), F32),
                        pltpu.VMEM((1, 1), F32), pltpu.VMEM((S, dv), F32)],
        compiler_params=_cparams("parallel", "parallel"),
        name="mlstm",
    )(qk, qk, u3, u3, grow, gcol, norm_g.reshape(1, H * dv).astype(F32))


def _hy_mlp_body(z_ref, w1_ref, b1_ref, w2_ref, b2_ref, fr_ref, o_ref):
    fr = fr_ref[...]
    h = jnp.sin(fr * (jnp.dot(z_ref[...], w1_ref[...], precision=HI, preferred_element_type=F32) + b1_ref[...]))
    h = jnp.sin(fr * (jnp.dot(h, w2_ref[...], precision=HI, preferred_element_type=F32) + b2_ref[...]))
    o_ref[...] = h


def _hy_filter_body(h_ref, wf_ref, wb_ref, bf_ref, bb_ref, df_ref, db_ref, hs_ref, hd_ref):
    L = h_ref.shape[0]
    h2 = h_ref[...]
    tc = wf_ref.shape[1]
    row = lax.broadcasted_iota(jnp.int32, (L, tc), 0)
    tn = row.astype(F32) / L
    hf = (jnp.dot(h2, wf_ref[...], precision=HI, preferred_element_type=F32) + bf_ref[...]) \
        * jnp.exp(-tn * jnp.abs(df_ref[...]))
    hb = (jnp.dot(h2, wb_ref[...], precision=HI, preferred_element_type=F32) + bb_ref[...]) \
        * jnp.exp(-tn * jnp.abs(db_ref[...]))
    hb = jnp.where(row == 0, 0.0, hb)
    norm = jnp.sum(jnp.abs(hf), axis=0, keepdims=True) + jnp.sum(jnp.abs(hb), axis=0, keepdims=True)
    inv = 1.0 / norm
    hs_ref[...] = ((hf + hb) * inv).astype(hs_ref.dtype)
    hd_ref[...] = ((hb - hf) * inv).astype(hd_ref.dtype)


def _hy_filters(L, w1, b1, w2, b2, w3, b3, freq, decay, tc=256):
    hid = w1.shape[1]
    C = w3.shape[1] // (2 * H_ORDER)
    tn = jnp.arange(L, dtype=F32) / L
    bands = jnp.linspace(1e-4, H_BANDS - 1, H_BANDS, dtype=F32)
    ang = (2.0 * math.pi) * tn[:, None] * bands[None, :]
    z = jnp.concatenate([tn[:, None], jnp.cos(ang), jnp.sin(ang)], axis=-1)
    emb = z.shape[1]
    z = jnp.pad(z, ((0, 0), (0, LANES - emb)))
    w1p = jnp.pad(w1.astype(F32), ((0, LANES - emb), (0, 0)))
    row = lambda a: a.reshape(1, -1).astype(F32)
    h2 = pl.pallas_call(
        _hy_mlp_body,
        out_shape=jax.ShapeDtypeStruct((L, hid), F32),
        compiler_params=pltpu.CompilerParams(vmem_limit_bytes=VMEM_LIMIT),
        name="hyena_mlp",
    )(z, w1p, row(b1), w2.astype(F32), row(b2), row(freq))
    tc = _tile(C, tc)
    nct = C // tc
    fwd = lambda o, j: (0, o * nct + j)
    bwd = lambda o, j: (0, (H_ORDER + o) * nct + j)
    out = lambda o, j: (0, o * nct + j)
    w3 = w3.astype(F32)
    b3r, dcr = row(b3), row(decay)
    return pl.pallas_call(
        _hy_filter_body,
        grid=(H_ORDER, nct),
        in_specs=[pl.BlockSpec((L, hid), lambda o, j: (0, 0)),
                  pl.BlockSpec((hid, tc), fwd), pl.BlockSpec((hid, tc), bwd),
                  pl.BlockSpec((1, tc), fwd), pl.BlockSpec((1, tc), bwd),
                  pl.BlockSpec((1, tc), fwd), pl.BlockSpec((1, tc), bwd)],
        out_specs=[pl.BlockSpec((L, tc), out), pl.BlockSpec((L, tc), out)],
        out_shape=[jax.ShapeDtypeStruct((L, H_ORDER * C), BF16)] * 2,
        compiler_params=_cparams("parallel", "parallel"),
        name="hyena_filter",
    )(h2, w3, w3, b3r, b3r, dcr, dcr)


def _dft_tables(L):
    k = jnp.arange(L, dtype=jnp.int32)[:, None]
    t = jnp.arange(L, dtype=jnp.int32)[None, :]
    m = ((2 * k + 1) * t) % (4 * L)
    ang = m.astype(F32) * F32(2.0 * math.pi / (4 * L))
    fc, fs = jnp.cos(ang), jnp.sin(ang)
    return fc.astype(BF16), fs.astype(BF16), fc.T.astype(BF16), fs.T.astype(BF16)


def _hy_fwd_body(fc_ref, fs_ref, z_ref, kre_ref, kim_ref, yre_ref, yim_ref):
    fc, fs = fc_ref[...], fs_ref[...]
    kre, kim = kre_ref[...], kim_ref[...]
    for b in range(z_ref.shape[0]):
        z = z_ref[b]
        zre = jnp.dot(fc, z, preferred_element_type=F32)
        zim = -jnp.dot(fs, z, preferred_element_type=F32)
        yre_ref[b] = (zre * kre - zim * kim).astype(yre_ref.dtype)
        yim_ref[b] = (zre * kim + zim * kre).astype(yim_ref.dtype)


def _hy_fwd(fc, fs, z3, z_off, kre, kim, order, C, tm=512, tn=256):
    B, L, _ = z3.shape
    tm, tn = _tile(L, tm), _tile(C, tn)
    nct = C // tn
    zo = z_off // tn
    assert z_off % tn == 0
    return pl.pallas_call(
        _hy_fwd_body,
        grid=(nct, L // tm),
        in_specs=[pl.BlockSpec((tm, L), lambda j, i: (i, 0)),
                  pl.BlockSpec((tm, L), lambda j, i: (i, 0)),
                  pl.BlockSpec((B, L, tn), lambda j, i: (0, 0, zo + j)),
                  pl.BlockSpec((tm, tn), lambda j, i: (i, order * nct + j)),
                  pl.BlockSpec((tm, tn), lambda j, i: (i, order * nct + j))],
        out_specs=[pl.BlockSpec((B, tm, tn), lambda j, i: (0, i, j))] * 2,
        out_shape=[jax.ShapeDtypeStruct((B, L, C), BF16)] * 2,
        compiler_params=_cparams("parallel", "parallel"),
        name="hyena_fwd",
    )(fc, fs, z3, kre, kim)


def _hy_inv_body(fct_ref, fst_ref, yre_ref, yim_ref, z_ref, g_ref, skip_ref, o_ref, *, scale):
    fct, fst = fct_ref[...], fst_ref[...]
    skip = skip_ref[...]
    for b in range(z_ref.shape[0]):
        y = jnp.dot(fct, yre_ref[b], preferred_element_type=F32) \
            - jnp.dot(fst, yim_ref[b], preferred_element_type=F32)
        z = z_ref[b].astype(F32)
        o_ref[b] = (g_ref[b].astype(F32) * (scale * y + skip * z)).astype(o_ref.dtype)


def _hy_inv(fct, fst, yre, yim, z3, z_off, g3, g_off, skip, tm=512, tn=256):
    B, L, C = yre.shape
    tm, tn = _tile(L, tm), _tile(C, tn)
    zo, go = z_off // tn, g_off // tn
    assert z_off % tn == 0 and g_off % tn == 0
    return pl.pallas_call(
        functools.partial(_hy_inv_body, scale=1.0 / L),
        grid=(C // tn, L // tm),
        in_specs=[pl.BlockSpec((tm, L), lambda j, i: (i, 0)),
                  pl.BlockSpec((tm, L), lambda j, i: (i, 0)),
                  pl.BlockSpec((B, L, tn), lambda j, i: (0, 0, j)),
                  pl.BlockSpec((B, L, tn), lambda j, i: (0, 0, j)),
                  pl.BlockSpec((B, tm, tn), lambda j, i: (0, i, zo + j)),
                  pl.BlockSpec((B, tm, tn), lambda j, i: (0, i, go + j)),
                  pl.BlockSpec((1, tn), lambda j, i: (0, j))],
        out_specs=pl.BlockSpec((B, tm, tn), lambda j, i: (0, i, j)),
        out_shape=jax.ShapeDtypeStruct((B, L, C), BF16),
        compiler_params=_cparams("parallel", "parallel"),
        name="hyena_inv",
    )(fct, fst, yre, yim, z3, g3, skip.reshape(1, C).astype(F32))


def _top_values(x, n):
    rank = lax.broadcasted_iota(jnp.int32, (n, x.shape[1]), 0)
    vals = jnp.zeros((n, x.shape[1]), F32)
    work = x
    for r in range(n):
        mx = jnp.max(work, axis=0, keepdims=True)
        vals = jnp.where(rank == r, mx, vals)
        work = jnp.where(work >= mx, -jnp.inf, work)
    return vals, mx


def _peer_route_body(q_ref, k_ref, s0_ref, s1_ref, e0_ref, e1_ref, tau_ref):
    nt = (((1,), (1,)), ((), ()))
    s0 = lax.dot_general(k_ref[0], q_ref[0], nt, preferred_element_type=F32)
    s1 = lax.dot_general(k_ref[1], q_ref[1], nt, preferred_element_type=F32)
    v0, _ = _top_values(s0, P_TOPK)
    v1, _ = _top_values(s1, P_TOPK)
    cand = jnp.concatenate([v0[a:a + 1, :] + v1 for a in range(P_TOPK)], axis=0)
    _, tau = _top_values(cand, P_TOPK)
    top = v0[0:1, :] + v1[0:1, :]
    zsum = jnp.sum(jnp.where(cand >= tau, jnp.exp(cand - top), 0.0), axis=0, keepdims=True)
    s0_ref[0] = s0
    s1_ref[0] = s1
    e0_ref[0] = jnp.exp(s0 - v0[0:1, :]) / zsum
    e1_ref[0] = jnp.exp(s1 - v1[0:1, :])
    tau_ref[0] = tau


def _peer_route(q3, keys, tm=256):
    H2, T, dk = q3.shape
    H = H2 // 2
    tm = _tile(T, tm)
    big = pl.BlockSpec((1, N_KEYS, tm), lambda i, h: (h, 0, i))
    shp = jax.ShapeDtypeStruct((H, N_KEYS, T), F32)
    return pl.pallas_call(
        _peer_route_body,
        grid=(T // tm, H),
        in_specs=[pl.BlockSpec((2, tm, dk), lambda i, h: (h, i, 0)),
                  pl.BlockSpec((2, N_KEYS, dk), lambda i, h: (h, 0, 0))],
        out_specs=[big, big, big, big, pl.BlockSpec((1, 1, tm), lambda i, h: (h, 0, i))],
        out_shape=[shp, shp, shp, shp, jax.ShapeDtypeStruct((H, 1, T), F32)],
        compiler_params=_cparams("parallel", "parallel"),
        name="peer_route",
    )(q3, keys)


def _peer_expert_body(xn_ref, u_ref, v_ref, s0_ref, s1_ref, e0_ref, e1_ref, tau_ref, x_ref, o_ref, *, te):
    e = pl.program_id(1)

    @pl.when(e == 0)
    def _():
        o_ref[...] = x_ref[...]

    a = lax.dot_general(u_ref[...], xn_ref[...], (((1,), (1,)), ((), ())), preferred_element_type=F32)
    act = 0.5 * a * (1.0 + lax.erf(a * (1.0 / math.sqrt(2.0))))
    parts = []
    for ii in range(te // N_KEYS):
        i_idx = e * (te // N_KEYS) + ii
        g = jnp.zeros((N_KEYS, a.shape[1]), F32)
        for h in range(P_HEADS):
            s0row = s0_ref[h, pl.ds(i_idx, 1), :]
            e0row = e0_ref[h, pl.ds(i_idx, 1), :]
            sel = (s0row + s1_ref[h]) >= tau_ref[h]
            g = g + jnp.where(sel, e0row * e1_ref[h], 0.0)
        parts.append((g * act[ii * N_KEYS:(ii + 1) * N_KEYS, :]).astype(BF16))
    w = jnp.concatenate(parts, axis=0) if len(parts) > 1 else parts[0]
    o_ref[...] += lax.dot_general(w, v_ref[...], (((0,), (0,)), ((), ())), preferred_element_type=F32)


def _peer_expert(xn, U, V, s0, s1, e0, e1, tau, x, tm=512, te=256):
    T, D = xn.shape
    E = U.shape[0]
    tm, te = _tile(T, tm), _tile(E, te)
    H = s0.shape[0]
    once = dict(pipeline_mode=pl.Buffered(1))
    rt = pl.BlockSpec((H, N_KEYS, tm), lambda i, e: (0, 0, i), **once)
    return pl.pallas_call(
        functools.partial(_peer_expert_body, te=te),
        grid=(T // tm, E // te),
        in_specs=[pl.BlockSpec((tm, D), lambda i, e: (i, 0), **once),
                  pl.BlockSpec((te, D), lambda i, e: (e, 0)),
                  pl.BlockSpec((te, D), lambda i, e: (e, 0)),
                  rt, rt, rt, rt,
                  pl.BlockSpec((H, 1, tm), lambda i, e: (0, 0, i), **once),
                  pl.BlockSpec((tm, D), lambda i, e: (i, 0), **once)],
        out_specs=pl.BlockSpec((tm, D), lambda i, e: (i, 0)),
        out_shape=jax.ShapeDtypeStruct((T, D), F32),
        compiler_params=_cparams("parallel", "arbitrary"),
        name="peer_expert",
    )(xn, U, V, s0, s1, e0, e1, tau, x)


def _layer(x2, B, S, p, tables):
    T, D = x2.shape
    H, dk, dv = M_HEADS, M_DQK, M_DV
    mqk, mv = H * dk, H * dv
    n_gate = 4 * H
    C = p["hy_skip"].shape[1]
    m_cols = 2 * mqk + 2 * mv + n_gate

    w_in = p["w_in"]
    w_main = jnp.concatenate([w_in[:, :m_cols - n_gate], w_in[:, m_cols:]], axis=1).astype(BF16)
    w_gate = jnp.pad(w_in[:, m_cols - n_gate:m_cols], ((0, 0), (0, LANES - n_gate))).astype(BF16)
    b_gate = jnp.pad(p["m_gate_b"], (0, LANES - n_gate))
    off_v, off_o = 2 * mqk, 2 * mqk + mv
    off_h = 2 * mqk + 2 * mv
    off_g = off_h + (H_ORDER + 1) * C

    xn = _rmsnorm(x2, p["ln1_g"], BF16)
    u = _matmul(xn, w_main, BF16, name="in_proj")
    gates = _matmul(xn, w_gate, F32, bias=b_gate, name="gate_proj")
    u3 = u.reshape(B, S, u.shape[1])

    qk_scale = jnp.concatenate([jnp.ones((mqk,), F32), jnp.full((mqk,), dk ** -0.5, F32)])
    qk = _dwconv(u3, 0, p["m_conv"], qk_scale, silu=True)
    g5 = gates[:, :n_gate].reshape(B, S, 4, H)
    grow = jnp.pad(jnp.transpose(g5, (0, 3, 2, 1)), ((0, 0), (0, 0), (0, 4), (0, 0)))
    gcol = jnp.pad(jnp.transpose(g5, (0, 3, 1, 2)), ((0, 0), (0, 0), (0, 0), (0, 4)))
    hm = _mlstm(qk, u3, off_v, off_o, grow, gcol, p["m_norm_g"])

    fc, fs, fct, fst = tables
    hyc = _dwconv(u3, off_h, p["hy_conv"], jnp.ones(((H_ORDER + 1) * C,), F32), silu=False)
    hsum, hdiff = _hy_filters(S, p["hy_w1"], p["hy_b1"], p["hy_w2"], p["hy_b2"], p["hy_w3"], p["hy_b3"],
                              p["hy_freq"], p["hy_decay"])
    kre = _matmul(fc, hsum, F32, name="filter_dft_re")
    kim = _matmul(fs, hdiff, F32, name="filter_dft_im")
    z3, z_off = hyc, 0
    for o in range(H_ORDER):
        yre, yim = _hy_fwd(fc, fs, z3, z_off, kre, kim, o, C)
        z3 = _hy_inv(fct, fst, yre, yim, z3, z_off, hyc, (o + 1) * C, p["hy_skip"][o])
        z_off = 0
    hy = z3

    merged = _merge_matmul(hm.reshape(T, mv), p["w_mo"].astype(BF16), hy.reshape(T, C), p["w_ho"].astype(BF16),
                           u, off_g, off_g + D)
    x2 = _matmul(merged, p["w_o"].astype(BF16), F32, residual=x2, name="out_proj")

    xn2 = _rmsnorm(x2, p["ln2_g"], BF16)
    q = _matmul(xn2, p["peer_wq"].astype(BF16), BF16, name="peer_query")
    q3 = jnp.transpose(q.reshape(T, 2 * P_HEADS, P_DKEY), (1, 0, 2))
    keys = p["peer_keys"].reshape(2 * P_HEADS, N_KEYS, P_DKEY).astype(BF16)
    s0, s1, e0, e1, tau = _peer_route(q3, keys)
    return _peer_expert(xn2, p["peer_u"].astype(BF16), p["peer_v"].astype(BF16), s0, s1, e0, e1, tau, x2)


_LAYER_PARAMS = ("ln1_g", "ln2_g", "w_in", "m_conv", "m_gate_b", "m_norm_g", "hy_conv", "hy_w1", "hy_b1",
                 "hy_w2", "hy_b2", "hy_w3", "hy_b3", "hy_freq", "hy_decay", "hy_skip", "w_mo", "w_ho", "w_o",
                 "peer_wq", "peer_keys", "peer_u", "peer_v")


def kernel(x, ln1_g, ln2_g, w_in, m_conv, m_gate_b, m_norm_g, hy_conv, hy_w1, hy_b1, hy_w2, hy_b2, hy_w3, hy_b3,
           hy_freq, hy_decay, hy_skip, w_mo, w_ho, w_o, peer_wq, peer_keys, peer_u, peer_v, final_g):
    stacked = dict(zip(_LAYER_PARAMS, (ln1_g, ln2_g, w_in, m_conv, m_gate_b, m_norm_g, hy_conv, hy_w1, hy_b1,
                                       hy_w2, hy_b2, hy_w3, hy_b3, hy_freq, hy_decay, hy_skip, w_mo, w_ho, w_o,
                                       peer_wq, peer_keys, peer_u, peer_v)))
    B, S, D = x.shape
    tables = _dft_tables(S)
    x2 = x.reshape(B * S, D)
    for l in range(ln1_g.shape[0]):
        x2 = _layer(x2, B, S, {k: v[l] for k, v in stacked.items()}, tables)
    return _rmsnorm(x2, final_g, F32).reshape(B, S, D)
```

```python
import functools
import math

import jax
import jax.numpy as jnp
from jax import lax
from jax.experimental import pallas as pl
from jax.experimental.pallas import tpu as pltpu

F32 = jnp.float32
BF16 = jnp.bfloat16

EPS = 1e-6
M_HEADS = 8
M_DQK = 256
M_DV = 512
M_CHUNK = 256
H_ORDER = 2
H_BANDS = 16
P_HEADS = 8
N_KEYS = 128
P_DKEY = 128
P_TOPK = 16

V7X_VMEM_BYTES = 64 * 1024 * 1024
VMEM_LIMIT = V7X_VMEM_BYTES - 8 * 1024 * 1024
LANES = 128
HI = lax.Precision.HIGHEST


def _cparams(*sem):
    return pltpu.CompilerParams(dimension_semantics=sem, vmem_limit_bytes=VMEM_LIMIT)


def _tile(dim, pref):
    t = min(dim, pref)
    while dim % t:
        t -= LANES
    assert t > 0 and dim % t == 0, (dim, pref)
    return t


def _rmsnorm_body(*refs, has_add, want_sum, want_t):
    x_ref, g_ref = refs[0], refs[1]
    pos = 2
    x = x_ref[...]
    if has_add:
        x = x + refs[pos][...].T
        pos += 1
    if want_sum:
        refs[pos][...] = x
        pos += 1
    ms = jnp.mean(x * x, axis=-1, keepdims=True)
    y = x * lax.rsqrt(ms + EPS) * g_ref[...]
    refs[pos][...] = y.astype(refs[pos].dtype)
    if want_t:
        refs[pos + 1][...] = y.T.astype(refs[pos + 1].dtype)


def _rmsnorm(x, g, out_dtype, add_t=None, want_sum=False, want_t=False):
    T, D = x.shape
    tm = _tile(T, 256)
    row = pl.BlockSpec((tm, D), lambda i: (i, 0))
    colT = pl.BlockSpec((D, tm), lambda i: (0, i))
    in_specs = [row, pl.BlockSpec((1, D), lambda i: (0, 0))]
    args = [x, g.reshape(1, D).astype(F32)]
    if add_t is not None:
        in_specs.append(colT)
        args.append(add_t)
    out_specs, out_shape = [], []
    if want_sum:
        out_specs.append(row)
        out_shape.append(jax.ShapeDtypeStruct((T, D), F32))
    out_specs.append(row)
    out_shape.append(jax.ShapeDtypeStruct((T, D), out_dtype))
    if want_t:
        out_specs.append(colT)
        out_shape.append(jax.ShapeDtypeStruct((D, T), out_dtype))
    return pl.pallas_call(
        functools.partial(_rmsnorm_body, has_add=add_t is not None, want_sum=want_sum, want_t=want_t),
        grid=(T // tm,),
        in_specs=in_specs, out_specs=out_specs, out_shape=out_shape,
        compiler_params=_cparams("parallel"),
        name="rmsnorm",
    )(*args)


def _mm_body(a_ref, b_ref, *rest, has_bias, has_res, cast_b):
    rest = list(rest)
    bq_ref = rest.pop() if cast_b else None
    o_ref = rest.pop()
    if cast_b:
        @pl.when(pl.program_id(1) == 0)
        def _():
            bq_ref[...] = b_ref[...].astype(BF16)
        b = bq_ref[...]
    else:
        b = b_ref[...]
    acc = jnp.dot(a_ref[...], b, preferred_element_type=F32)
    if has_bias:
        acc = acc + rest[0][...]
    if has_res:
        acc = acc + rest[1 if has_bias else 0][...]
    o_ref[...] = acc.astype(o_ref.dtype)


def _matmul(a, b, out_dtype, *, layer=None, n_cols=None, col_off=0, bias=None, residual=None,
            tm=512, tn=1024, name="matmul"):
    M, K = a.shape
    Nb = b.shape[-1]
    N = n_cols if n_cols is not None else Nb
    cast_b = b.dtype != BF16
    tm, tn = _tile(M, tm), _tile(N, tn)
    assert col_off % tn == 0
    co = col_off // tn
    n_outer = cast_b or a.size * (N // tn) + K * N < a.size + K * N * (M // tm)
    if n_outer:
        grid = (N // tn, M // tm)
        im = lambda j, i: (i, 0)
        jn = lambda j, i: (0, j)
        ij = lambda j, i: (i, j)
        jb2 = lambda j, i: (0, co + j)
        jb3 = lambda j, i: (layer, 0, co + j)
    else:
        grid = (M // tm, N // tn)
        im = lambda i, j: (i, 0)
        jn = lambda i, j: (0, j)
        ij = lambda i, j: (i, j)
        jb2 = lambda i, j: (0, co + j)
        jb3 = lambda i, j: (layer, 0, co + j)
    b_spec = pl.BlockSpec((K, tn), jb2) if b.ndim == 2 else pl.BlockSpec((None, K, tn), jb3)
    in_specs = [pl.BlockSpec((tm, K), im), b_spec]
    args = [a, b]
    if bias is not None:
        in_specs.append(pl.BlockSpec((1, tn), jn))
        args.append(bias.reshape(1, N).astype(F32))
    if residual is not None:
        in_specs.append(pl.BlockSpec((tm, tn), ij))
        args.append(residual)
    return pl.pallas_call(
        functools.partial(_mm_body, has_bias=bias is not None, has_res=residual is not None, cast_b=cast_b),
        grid=grid,
        in_specs=in_specs,
        out_specs=pl.BlockSpec((tm, tn), ij),
        out_shape=jax.ShapeDtypeStruct((M, N), out_dtype),
        scratch_shapes=[pltpu.VMEM((K, tn), BF16)] if cast_b else [],
        compiler_params=_cparams("parallel", "arbitrary" if cast_b else "parallel"),
        name=name,
    )(*args)


def _merge_body(a1_ref, b1_ref, a2_ref, b2_ref, g1_ref, g2_ref, o_ref):
    y1 = jnp.dot(a1_ref[...], b1_ref[...], preferred_element_type=F32)
    y2 = jnp.dot(a2_ref[...], b2_ref[...], preferred_element_type=F32)
    g1 = jax.nn.sigmoid(g1_ref[...].astype(F32))
    g2 = jax.nn.sigmoid(g2_ref[...].astype(F32))
    o_ref[...] = (g1 * y1 + g2 * y2).astype(o_ref.dtype)


def _merge_matmul(a1, b1, a2, b2, u, g1_off, g2_off, tm=512, tn=512):
    M, K = a1.shape
    _, N = b1.shape
    tm, tn = _tile(M, tm), _tile(N, tn)
    o1, o2 = g1_off // tn, g2_off // tn
    assert g1_off % tn == 0 and g2_off % tn == 0
    return pl.pallas_call(
        _merge_body,
        grid=(M // tm, N // tn),
        in_specs=[pl.BlockSpec((tm, K), lambda i, j: (i, 0)),
                  pl.BlockSpec((K, tn), lambda i, j: (0, j)),
                  pl.BlockSpec((tm, K), lambda i, j: (i, 0)),
                  pl.BlockSpec((K, tn), lambda i, j: (0, j)),
                  pl.BlockSpec((tm, tn), lambda i, j: (i, o1 + j)),
                  pl.BlockSpec((tm, tn), lambda i, j: (i, o2 + j))],
        out_specs=pl.BlockSpec((tm, tn), lambda i, j: (i, j)),
        out_shape=jax.ShapeDtypeStruct((M, N), BF16),
        compiler_params=_cparams("parallel", "parallel"),
        name="merge_matmul",
    )(a1, b1, a2, b2, u, u)


def _dwconv_body(x_ref, w_ref, s_ref, o_ref, *, silu):
    x = x_ref[0].astype(F32)
    S = x.shape[0]
    row = lax.broadcasted_iota(jnp.int32, x.shape, 0)
    xm = jnp.where(row == 0, 0.0, pltpu.roll(x, 1, axis=0))
    xp = jnp.where(row == S - 1, 0.0, pltpu.roll(x, S - 1, axis=0))
    w = w_ref[...]
    y = w[0:1, :] * xm + w[1:2, :] * x + w[2:3, :] * xp
    if silu:
        y = y * jax.nn.sigmoid(y)
    o_ref[0] = (y * s_ref[...]).astype(o_ref.dtype)


def _dwconv(u3, col_off, w, scale, silu, tc=256):
    B, S, _ = u3.shape
    C = w.shape[1]
    tc = _tile(C, tc)
    assert col_off % tc == 0
    o = col_off // tc
    return pl.pallas_call(
        functools.partial(_dwconv_body, silu=silu),
        grid=(B, C // tc),
        in_specs=[pl.BlockSpec((1, S, tc), lambda b, j: (b, 0, o + j)),
                  pl.BlockSpec((3, tc), lambda b, j: (0, j)),
                  pl.BlockSpec((1, tc), lambda b, j: (0, j))],
        out_specs=pl.BlockSpec((1, S, tc), lambda b, j: (b, 0, j)),
        out_shape=jax.ShapeDtypeStruct((B, S, C), BF16),
        compiler_params=_cparams("parallel", "parallel"),
        name="dwconv_silu" if silu else "dwconv",
    )(u3, w.astype(F32), scale.reshape(1, C).astype(F32))


def _log_sigmoid(x):
    return jnp.minimum(x, 0.0) - jnp.log1p(jnp.exp(-jnp.abs(x)))


def _mlstm_body(q_ref, k_ref, v_ref, o_ref, grow_ref, gcol_ref, ng_ref, out_ref,
                c_ref, n_ref, m_ref, h_ref, *, L):
    S = q_ref.shape[1]
    nc = S // L
    row = lax.broadcasted_iota(jnp.int32, (L, L), 0)
    col = lax.broadcasted_iota(jnp.int32, (L, L), 1)

    def direction(rev):
        gi, gf = (2, 3) if rev else (0, 1)
        keep = (col >= row) if rev else (col <= row)
        cum_rs = ((row >= col) if rev else (row <= col)).astype(F32)
        cum_tr = keep.astype(F32)
        last = 0 if rev else L - 1

        c_ref[...] = jnp.zeros_like(c_ref)
        n_ref[...] = jnp.zeros_like(n_ref)
        m_ref[...] = jnp.zeros_like(m_ref)

        def chunk(c, carry):
            cc = (nc - 1 - c) if rev else c
            r0 = pl.multiple_of(cc * L, L)
            q = q_ref[0, pl.ds(r0, L), :]
            k = k_ref[0, pl.ds(r0, L), :]
            v = v_ref[0, pl.ds(r0, L), :]
            grow = grow_ref[0, 0, :, pl.ds(r0, L)]
            gcol = gcol_ref[0, 0, pl.ds(r0, L), :]
            b_rows = jnp.dot(_log_sigmoid(grow), cum_rs, precision=HI, preferred_element_type=F32)
            b_cols = jnp.dot(cum_tr, _log_sigmoid(gcol), precision=HI, preferred_element_type=F32)
            b_r = b_rows[gf:gf + 1, :]
            b_c = b_cols[:, gf:gf + 1]
            li_r = grow[gi:gi + 1, :]
            li_c = gcol[:, gi:gi + 1]
            m_prev = m_ref[...]

            dlog = jnp.where(keep, b_c - b_r + li_r, -jnp.inf)
            inter = b_c + m_prev
            m_t = jnp.maximum(inter, jnp.max(dlog, axis=1, keepdims=True))
            w_intra = jnp.exp(dlog - m_t)
            w_inter = jnp.exp(inter - m_t)
            s = lax.dot_general(q, k, (((1,), (1,)), ((), ())), preferred_element_type=F32) * w_intra
            cmat = c_ref[...]
            num = w_inter * jnp.dot(q, cmat.astype(BF16), preferred_element_type=F32) \
                + jnp.dot(s.astype(BF16), v, preferred_element_type=F32)
            qn = jnp.sum(q.astype(F32) * n_ref[...], axis=1, keepdims=True)
            den = w_inter * qn + jnp.sum(s, axis=1, keepdims=True)
            h = num / jnp.maximum(jnp.abs(den), jnp.exp(-m_t))

            b_last = b_r[:, last:last + 1]
            gl = b_last - b_c + li_c
            m_new = jnp.maximum(b_last + m_prev, jnp.max(gl, axis=0, keepdims=True))
            dec = jnp.exp(b_last + m_prev - m_new)
            kw = k.astype(F32) * jnp.exp(gl - m_new)
            c_ref[...] = dec * cmat + lax.dot_general(
                kw.astype(BF16), v, (((0,), (0,)), ((), ())), preferred_element_type=F32)
            n_ref[...] = dec * n_ref[...] + jnp.sum(kw, axis=0, keepdims=True)
            m_ref[...] = m_new

            if not rev:
                h_ref[pl.ds(r0, L), :] = h
            else:
                hs = h_ref[pl.ds(r0, L), :] + h
                hn = hs * lax.rsqrt(jnp.mean(hs * hs, axis=1, keepdims=True) + EPS)
                gate = jax.nn.sigmoid(o_ref[0, pl.ds(r0, L), :].astype(F32))
                out_ref[0, pl.ds(r0, L), :] = (hn * ng_ref[...] * gate).astype(out_ref.dtype)
            return carry

        lax.fori_loop(0, nc, chunk, 0)

    direction(False)
    direction(True)


def _mlstm(qk, u3, v_off, o_off, grow, gcol, norm_g):
    B, S, _ = qk.shape
    H, dk, dv = M_HEADS, M_DQK, M_DV
    L = _tile(S, M_CHUNK)
    vo, oo = v_off // dv, o_off // dv
    assert v_off % dv == 0 and o_off % dv == 0
    return pl.pallas_call(
        functools.partial(_mlstm_body, L=L),
        grid=(B, H),
        in_specs=[pl.BlockSpec((1, S, dk), lambda b, h: (b, 0, h)),
                  pl.BlockSpec((1, S, dk), lambda b, h: (b, 0, H + h)),
                  pl.BlockSpec((1, S, dv), lambda b, h: (b, 0, vo + h)),
                  pl.BlockSpec((1, S, dv), lambda b, h: (b, 0, oo + h)),
                  pl.BlockSpec((1, 1, 8, S), lambda b, h: (b, h, 0, 0)),
                  pl.BlockSpec((1, 1, S, 8), lambda b, h: (b, h, 0, 0)),
                  pl.BlockSpec((1, dv), lambda b, h: (0, h))],
        out_specs=pl.BlockSpec((1, S, dv), lambda b, h: (b, 0, h)),
        out_shape=jax.ShapeDtypeStruct((B, S, H * dv), BF16),
        scratch_shapes=[pltpu.VMEM((dk, dv), F32), pltpu.VMEM((1, dk), F32),
                        pltpu.VMEM((1, 1), F32), pltpu.VMEM((S, dv), F32)],
        compiler_params=_cparams("parallel", "parallel"),
        name="mlstm",
    )(qk, qk, u3, u3, grow, gcol, norm_g.reshape(1, H * dv).astype(F32))


def _hy_mlp_body(z_ref, w1_ref, b1_ref, w2_ref, b2_ref, fr_ref, o_ref):
    fr = fr_ref[...]
    h = jnp.sin(fr * (jnp.dot(z_ref[...], w1_ref[...], precision=HI, preferred_element_type=F32) + b1_ref[...]))
    h = jnp.sin(fr * (jnp.dot(h, w2_ref[...], precision=HI, preferred_element_type=F32) + b2_ref[...]))
    o_ref[...] = h


def _hy_filter_body(h_ref, wf_ref, wb_ref, bf_ref, bb_ref, df_ref, db_ref, hs_ref, hd_ref):
    L = h_ref.shape[0]
    h2 = h_ref[...]
    tc = wf_ref.shape[1]
    row = lax.broadcasted_iota(jnp.int32, (L, tc), 0)
    tn = row.astype(F32) / L
    hf = (jnp.dot(h2, wf_ref[...], precision=HI, preferred_element_type=F32) + bf_ref[...]) \
        * jnp.exp(-tn * jnp.abs(df_ref[...]))
    hb = (jnp.dot(h2, wb_ref[...], precision=HI, preferred_element_type=F32) + bb_ref[...]) \
        * jnp.exp(-tn * jnp.abs(db_ref[...]))
    hb = jnp.where(row == 0, 0.0, hb)
    norm = jnp.sum(jnp.abs(hf), axis=0, keepdims=True) + jnp.sum(jnp.abs(hb), axis=0, keepdims=True)
    inv = 1.0 / norm
    hs_ref[...] = ((hf + hb) * inv).astype(hs_ref.dtype)
    hd_ref[...] = ((hb - hf) * inv).astype(hd_ref.dtype)


def _hy_filters(L, w1, b1, w2, b2, w3, b3, freq, decay, tc=256):
    hid = w1.shape[1]
    C = w3.shape[1] // (2 * H_ORDER)
    tn = jnp.arange(L, dtype=F32) / L
    bands = jnp.linspace(1e-4, H_BANDS - 1, H_BANDS, dtype=F32)
    ang = (2.0 * math.pi) * tn[:, None] * bands[None, :]
    z = jnp.concatenate([tn[:, None], jnp.cos(ang), jnp.sin(ang)], axis=-1)
    emb = z.shape[1]
    z = jnp.pad(z, ((0, 0), (0, LANES - emb)))
    w1p = jnp.pad(w1.astype(F32), ((0, LANES - emb), (0, 0)))
    row = lambda a: a.reshape(1, -1).astype(F32)
    h2 = pl.pallas_call(
        _hy_mlp_body,
        out_shape=jax.ShapeDtypeStruct((L, hid), F32),
        compiler_params=pltpu.CompilerParams(vmem_limit_bytes=VMEM_LIMIT),
        name="hyena_mlp",
    )(z, w1p, row(b1), w2.astype(F32), row(b2), row(freq))
    tc = _tile(C, tc)
    nct = C // tc
    fwd = lambda o, j: (0, o * nct + j)
    bwd = lambda o, j: (0, (H_ORDER + o) * nct + j)
    out = lambda o, j: (0, o * nct + j)
    w3 = w3.astype(F32)
    b3r, dcr = row(b3), row(decay)
    return pl.pallas_call(
        _hy_filter_body,
        grid=(H_ORDER, nct),
        in_specs=[pl.BlockSpec((L, hid), lambda o, j: (0, 0)),
                  pl.BlockSpec((hid, tc), fwd), pl.BlockSpec((hid, tc), bwd),
                  pl.BlockSpec((1, tc), fwd), pl.BlockSpec((1, tc), bwd),
                  pl.BlockSpec((1, tc), fwd), pl.BlockSpec((1, tc), bwd)],
        out_specs=[pl.BlockSpec((L, tc), out), pl.BlockSpec((L, tc), out)],
        out_shape=[jax.ShapeDtypeStruct((L, H_ORDER * C), BF16)] * 2,
        compiler_params=_cparams("parallel", "parallel"),
        name="hyena_filter",
    )(h2, w3, w3, b3r, b3r, dcr, dcr)


def _dft_tables(L):
    L2 = L // 2
    kap = jnp.arange(L2, dtype=jnp.int32)
    m = ((2 * kap[:, None] + 1) * kap[None, :]) % (2 * L)
    ang = m.astype(F32) * F32(math.pi / L)
    gc, gs = jnp.cos(ang), jnp.sin(ang)
    w_lo = (2 * kap + 1).astype(F32) * F32(math.pi / (2 * L))
    w_hi = (2 * (L - 1 - kap) + 1).astype(F32) * F32(math.pi / (2 * L))
    tw = jnp.stack([jnp.cos(w_lo), jnp.sin(w_lo), jnp.cos(w_hi), jnp.sin(w_hi)], axis=1)
    return gc.astype(BF16), gs.astype(BF16), gc.T.astype(BF16), gs.T.astype(BF16), tw


def _twiddles(tw_ref):
    tw = tw_ref[...]
    return tw[:, 0:1], tw[:, 1:2], tw[:, 2:3], tw[:, 3:4]


def _hy_kspec_body(gc_ref, gs_ref, ae_ref, ao_ref, be_ref, bo_ref, tw_ref, kre_ref, kim_ref):
    gc, gs = gc_ref[...], gs_ref[...]
    c, s, ch, sh = _twiddles(tw_ref)
    dot = functools.partial(jnp.dot, preferred_element_type=F32)
    p1, p2, p3 = dot(gc, ae_ref[...]), dot(gc, ao_ref[...]), dot(gs, ao_ref[...])
    p4, p5, p6 = dot(gs, be_ref[...]), dot(gc, bo_ref[...]), dot(gs, bo_ref[...])
    kre_ref[0] = p1 + c * p2 - s * p3
    kim_ref[0] = p4 + s * p5 + c * p6
    kre_ref[1] = p1 + ch * p2 + sh * p3
    kim_ref[1] = -p4 + sh * p5 - ch * p6


def _hy_kspec(gc, gs, tw, hsum, hdiff, tm=512, tn=256):
    L, OC = hsum.shape
    L2 = L // 2
    a2, b2 = hsum.reshape(L2, 2 * OC), hdiff.reshape(L2, 2 * OC)
    tm, tn = _tile(L2, tm), _tile(OC, tn)
    nct = OC // tn
    tab = pl.BlockSpec((tm, L2), lambda j, i: (i, 0))
    even = pl.BlockSpec((L2, tn), lambda j, i: (0, j))
    odd = pl.BlockSpec((L2, tn), lambda j, i: (0, nct + j))
    out = pl.BlockSpec((2, tm, tn), lambda j, i: (0, i, j))
    shp = jax.ShapeDtypeStruct((2, L2, OC), F32)
    return pl.pallas_call(
        _hy_kspec_body,
        grid=(nct, L2 // tm),
        in_specs=[tab, tab, even, odd, even, odd, pl.BlockSpec((tm, 4), lambda j, i: (i, 0))],
        out_specs=[out, out], out_shape=[shp, shp],
        compiler_params=_cparams("parallel", "parallel"),
        name="hyena_kspec",
    )(gc, gs, a2, a2, b2, b2, tw)


def _hy_fwd_body(gc_ref, gs_ref, ze_ref, zo_ref, kre_ref, kim_ref, tw_ref, wre_ref, wim_ref):
    gc, gs = gc_ref[...], gs_ref[...]
    c, s, ch, sh = _twiddles(tw_ref)
    kre_l, kim_l, kre_h, kim_h = kre_ref[0], kim_ref[0], kre_ref[1], kim_ref[1]
    dot = functools.partial(jnp.dot, preferred_element_type=F32)
    for b in range(ze_ref.shape[0]):
        r0, s0 = dot(gc, ze_ref[b]), dot(gs, ze_ref[b])
        r1, s1 = dot(gc, zo_ref[b]), dot(gs, zo_ref[b])
        zre_l = r0 + c * r1 - s * s1
        zim_l = -s0 - c * s1 - s * r1
        zre_h = r0 + ch * r1 + sh * s1
        zim_h = s0 + ch * s1 - sh * r1
        a = zre_l * kre_l - zim_l * kim_l
        bb = zre_l * kim_l + zim_l * kre_l
        p = zre_h * kre_h - zim_h * kim_h
        q = zre_h * kim_h + zim_h * kre_h
        wre_ref[b, 0] = (a + p).astype(wre_ref.dtype)
        wim_ref[b, 0] = (bb - q).astype(wim_ref.dtype)
        wre_ref[b, 1] = (a * c - bb * s + p * ch - q * sh).astype(wre_ref.dtype)
        wim_ref[b, 1] = (a * s + bb * c - p * sh - q * ch).astype(wim_ref.dtype)


def _hy_fwd(gc, gs, tw, ze_arr, ze_off, zo_arr, zo_off, kre, kim, order, C, tm=512, tn=256):
    B, L2, _ = ze_arr.shape
    tm, tn = _tile(L2, tm), _tile(C, tn)
    nct = C // tn
    eo, oo = ze_off // tn, zo_off // tn
    assert ze_off % tn == 0 and zo_off % tn == 0
    tab = pl.BlockSpec((tm, L2), lambda j, i: (i, 0))
    ksp = pl.BlockSpec((2, tm, tn), lambda j, i: (0, i, order * nct + j))
    out = pl.BlockSpec((B, 2, tm, tn), lambda j, i: (0, 0, i, j))
    shp = jax.ShapeDtypeStruct((B, 2, L2, C), BF16)
    return pl.pallas_call(
        _hy_fwd_body,
        grid=(nct, L2 // tm),
        in_specs=[tab, tab,
                  pl.BlockSpec((B, L2, tn), lambda j, i: (0, 0, eo + j)),
                  pl.BlockSpec((B, L2, tn), lambda j, i: (0, 0, oo + j)),
                  ksp, ksp, pl.BlockSpec((tm, 4), lambda j, i: (i, 0))],
        out_specs=[out, out], out_shape=[shp, shp],
        compiler_params=_cparams("parallel", "parallel"),
        name="hyena_fwd",
    )(gc, gs, ze_arr, zo_arr, kre, kim, tw)


def _hy_inv_body(gct_ref, gst_ref, wre_ref, wim_ref, ze_ref, zo_ref, ge_ref, go_ref, skip_ref,
                 oe_ref, oo_ref, *, scale):
    gct, gst = gct_ref[...], gst_ref[...]
    skip = skip_ref[...]
    dot = functools.partial(jnp.dot, preferred_element_type=F32)
    for b in range(ze_ref.shape[0]):
        for j, (z_ref, g_ref, o_ref) in enumerate(((ze_ref, ge_ref, oe_ref), (zo_ref, go_ref, oo_ref))):
            y = dot(gct, wre_ref[b, j]) - dot(gst, wim_ref[b, j])
            o_ref[b] = (g_ref[b].astype(F32) * (scale * y + skip * z_ref[b].astype(F32))).astype(o_ref.dtype)


def _hy_inv(gct, gst, wre, wim, ze_arr, ze_off, zo_arr, zo_off, g_arr, ge_off, go_off, skip, tm=512, tn=256):
    B, _, L2, C = wre.shape
    tm, tn = _tile(L2, tm), _tile(C, tn)
    offs = [o // tn for o in (ze_off, zo_off, ge_off, go_off)]
    assert all(o % tn == 0 for o in (ze_off, zo_off, ge_off, go_off))
    tab = pl.BlockSpec((tm, L2), lambda j, i: (i, 0))
    wsp = pl.BlockSpec((B, 2, L2, tn), lambda j, i: (0, 0, 0, j))
    rows = lambda off: pl.BlockSpec((B, tm, tn), lambda j, i: (0, i, off + j))
    shp = jax.ShapeDtypeStruct((B, L2, C), BF16)
    return pl.pallas_call(
        functools.partial(_hy_inv_body, scale=1.0 / (2 * L2)),
        grid=(C // tn, L2 // tm),
        in_specs=[tab, tab, wsp, wsp, rows(offs[0]), rows(offs[1]), rows(offs[2]), rows(offs[3]),
                  pl.BlockSpec((1, tn), lambda j, i: (0, j))],
        out_specs=[rows(0), rows(0)], out_shape=[shp, shp],
        compiler_params=_cparams("parallel", "parallel"),
        name="hyena_inv",
    )(gct, gst, wre, wim, ze_arr, zo_arr, g_arr, g_arr, skip.reshape(1, C).astype(F32))


def _top_values(x, n):
    rank = lax.broadcasted_iota(jnp.int32, (n, x.shape[1]), 0)
    vals = jnp.zeros((n, x.shape[1]), F32)
    work = x
    for r in range(n):
        mx = jnp.max(work, axis=0, keepdims=True)
        vals = jnp.where(rank == r, mx, vals)
        work = jnp.where(work >= mx, -jnp.inf, work)
    return vals, mx


def _peer_route_body(q_ref, k_ref, s1_ref, e1_ref, th_ref, e0_ref):
    nt = (((1,), (1,)), ((), ()))
    s0 = lax.dot_general(k_ref[0], q_ref[0], nt, preferred_element_type=F32)
    s1 = lax.dot_general(k_ref[1], q_ref[1], nt, preferred_element_type=F32)
    v0, _ = _top_values(s0, P_TOPK)
    v1, _ = _top_values(s1, P_TOPK)
    cand = jnp.concatenate([v0[a:a + 1, :] + v1 for a in range(P_TOPK)], axis=0)
    _, tau = _top_values(cand, P_TOPK)
    top = v0[0:1, :] + v1[0:1, :]
    zsum = jnp.sum(jnp.where(cand >= tau, jnp.exp(cand - top), 0.0), axis=0, keepdims=True)
    th = jnp.full(s0.shape, jnp.inf, F32)
    for b in range(P_TOPK):
        vb = v1[b:b + 1, :]
        th = jnp.where(s0 + vb >= tau, vb, th)
    s1_ref[0] = s1
    e1_ref[0] = jnp.exp(s1 - v1[0:1, :])
    th_ref[0] = th
    e0_ref[0] = jnp.exp(s0 - v0[0:1, :]) / zsum


def _peer_route(q3, keys, tm=256):
    H2, T, dk = q3.shape
    H = H2 // 2
    tm = _tile(T, tm)
    big = pl.BlockSpec((1, N_KEYS, tm), lambda i, h: (h, 0, i))
    shp = jax.ShapeDtypeStruct((H, N_KEYS, T), F32)
    return pl.pallas_call(
        _peer_route_body,
        grid=(T // tm, H),
        in_specs=[pl.BlockSpec((2, tm, dk), lambda i, h: (h, i, 0)),
                  pl.BlockSpec((2, N_KEYS, dk), lambda i, h: (h, 0, 0))],
        out_specs=[big] * 4, out_shape=[shp] * 4,
        compiler_params=_cparams("parallel", "parallel"),
        name="peer_route",
    )(q3, keys)


GATE_ROWS = 32


def _peer_expert_body(xt_ref, u_ref, vt_ref, s1_ref, e1_ref, th_ref, e0_ref, o_ref, a_ref, w_ref, *, te):
    e = pl.program_id(1)

    @pl.when(e == 0)
    def _():
        o_ref[...] = jnp.zeros_like(o_ref)

    tm = a_ref.shape[1]
    half = tm // 2 if tm % (2 * LANES) == 0 else tm
    halves = [slice(s, s + half) for s in range(0, tm, half)]

    rows_i = []
    for ii in range(te // N_KEYS):
        i_idx = e * (te // N_KEYS) + ii
        rows_i.append(([th_ref[h, pl.ds(i_idx, 1), :] for h in range(P_HEADS)],
                       [e0_ref[h, pl.ds(i_idx, 1), :] for h in range(P_HEADS)]))

    def scores(hs):
        a_ref[:, hs] = jnp.dot(u_ref[...], xt_ref[:, hs], preferred_element_type=F32)

    def weights(hs):
        for ii in range(te // N_KEYS):
            th_full, e0_full = rows_i[ii]
            for tc in range(hs.start // LANES, hs.stop // LANES):
                cols = slice(tc * LANES, (tc + 1) * LANES)
                th_rows = [r[:, cols] for r in th_full]
                e0_rows = [r[:, cols] for r in e0_full]
                for jb in range(N_KEYS // GATE_ROWS):
                    rows = slice(jb * GATE_ROWS, (jb + 1) * GATE_ROWS)
                    g = None
                    for h in range(P_HEADS):
                        term = e0_rows[h] * jnp.where(s1_ref[h, rows, cols] >= th_rows[h],
                                                      e1_ref[h, rows, cols], 0.0)
                        g = term if g is None else g + term
                    erows = slice(ii * N_KEYS + jb * GATE_ROWS, ii * N_KEYS + (jb + 1) * GATE_ROWS)
                    a = a_ref[erows, cols]
                    act = 0.5 * a * (1.0 + lax.erf(a * (1.0 / math.sqrt(2.0))))
                    w_ref[erows, cols] = (g * act).astype(BF16)

    def project(hs):
        o_ref[:, hs] += jnp.dot(vt_ref[...], w_ref[:, hs], preferred_element_type=F32)

    for hs in halves:
        scores(hs)
        weights(hs)
    for hs in halves:
        project(hs)


def _peer_expert(xt, U, Vt, s1, e1, th, e0, tm=512, te=512):
    D, T = xt.shape
    E = U.shape[0]
    tm, te = _tile(T, tm), _tile(E, te)
    H = s1.shape[0]
    once = dict(pipeline_mode=pl.Buffered(1))
    rt = pl.BlockSpec((H, N_KEYS, tm), lambda i, e: (0, 0, i), **once)
    return pl.pallas_call(
        functools.partial(_peer_expert_body, te=te),
        grid=(T // tm, E // te),
        in_specs=[pl.BlockSpec((D, tm), lambda i, e: (0, i), **once),
                  pl.BlockSpec((te, D), lambda i, e: (e, 0)),
                  pl.BlockSpec((D, te), lambda i, e: (0, e)),
                  rt, rt, rt, rt],
        out_specs=pl.BlockSpec((D, tm), lambda i, e: (0, i)),
        out_shape=jax.ShapeDtypeStruct((D, T), F32),
        scratch_shapes=[pltpu.VMEM((te, tm), F32), pltpu.VMEM((te, tm), BF16)],
        compiler_params=_cparams("parallel", "arbitrary"),
        name="peer_expert",
    )(xt, U, Vt, s1, e1, th, e0)


def _layer(l, x2, xn, B, S, P, tables):
    T, D = x2.shape
    H, dk, dv = M_HEADS, M_DQK, M_DV
    mqk, mv = H * dk, H * dv
    n_gate = 4 * H
    C = P["hy_skip"].shape[2]
    m_cols = 2 * mqk + 2 * mv + n_gate
    assert (m_cols - n_gate) % LANES == 0
    w_in = P["w_in"]

    u_m = _matmul(xn, w_in, BF16, layer=l, n_cols=m_cols - n_gate, tn=512, name="in_proj_mlstm")
    b_gate = jnp.pad(P["m_gate_b"][l], (0, LANES - n_gate))
    gates = _matmul(xn, w_in, F32, layer=l, n_cols=LANES, col_off=m_cols - n_gate, bias=b_gate, name="gate_proj")
    u_hg = _matmul(xn, w_in[l, :, m_cols:].astype(BF16), BF16, name="in_proj_hyena")
    um3 = u_m.reshape(B, S, u_m.shape[1])

    qk_scale = jnp.concatenate([jnp.ones((mqk,), F32), jnp.full((mqk,), dk ** -0.5, F32)])
    qk = _dwconv(um3, 0, P["m_conv"][l], qk_scale, silu=True)
    g5 = gates[:, :n_gate].reshape(B, S, 4, H)
    grow = jnp.pad(jnp.transpose(g5, (0, 3, 2, 1)), ((0, 0), (0, 0), (0, 4), (0, 0)))
    gcol = jnp.pad(jnp.transpose(g5, (0, 3, 1, 2)), ((0, 0), (0, 0), (0, 0), (0, 4)))
    hm = _mlstm(qk, um3, 2 * mqk, 2 * mqk + mv, grow, gcol, P["m_norm_g"][l])

    gc, gs, gct, gst, tw = tables
    nh = (H_ORDER + 1) * C
    hyc = _dwconv(u_hg.reshape(B, S, u_hg.shape[1]), 0, P["hy_conv"][l], jnp.ones((nh,), F32), silu=False)
    hy2 = hyc.reshape(B, S // 2, 2 * nh)
    hsum, hdiff = _hy_filters(S, P["hy_w1"][l], P["hy_b1"][l], P["hy_w2"][l], P["hy_b2"][l], P["hy_w3"][l],
                              P["hy_b3"][l], P["hy_freq"][l], P["hy_decay"][l])
    kre, kim = _hy_kspec(gc, gs, tw, hsum, hdiff)
    ze, ze_off, zo, zo_off = hy2, 0, hy2, nh
    for o in range(H_ORDER):
        wre, wim = _hy_fwd(gc, gs, tw, ze, ze_off, zo, zo_off, kre, kim, o, C)
        ze, zo = _hy_inv(gct, gst, wre, wim, ze, ze_off, zo, zo_off, hy2, (o + 1) * C, nh + (o + 1) * C,
                         P["hy_skip"][l, o])
        ze_off = zo_off = 0
    hy = jnp.stack([ze, zo], axis=2).reshape(T, C)

    merged = _merge_matmul(hm.reshape(T, mv), P["w_mo"][l].astype(BF16), hy, P["w_ho"][l].astype(BF16),
                           u_hg, nh, nh + D)
    x2 = _matmul(merged, P["w_o"], F32, layer=l, residual=x2, tn=512, name="out_proj")

    xn2, xn2t = _rmsnorm(x2, P["ln2_g"][l], BF16, want_t=True)
    q = _matmul(xn2, P["peer_wq"], BF16, layer=l, tn=512, name="peer_query")
    q3 = jnp.transpose(q.reshape(T, 2 * P_HEADS, P_DKEY), (1, 0, 2))
    keys = P["peer_keys"][l].reshape(2 * P_HEADS, N_KEYS, P_DKEY).astype(BF16)
    s1, e1, th, e0 = _peer_route(q3, keys)
    yt = _peer_expert(xn2t, P["peer_u"][l].astype(BF16), P["peer_v"][l].T.astype(BF16), s1, e1, th, e0)
    return x2, yt


_LAYER_PARAMS = ("ln1_g", "ln2_g", "w_in", "m_conv", "m_gate_b", "m_norm_g", "hy_conv", "hy_w1", "hy_b1",
                 "hy_w2", "hy_b2", "hy_w3", "hy_b3", "hy_freq", "hy_decay", "hy_skip", "w_mo", "w_ho", "w_o",
                 "peer_wq", "peer_keys", "peer_u", "peer_v")


def kernel(x, ln1_g, ln2_g, w_in, m_conv, m_gate_b, m_norm_g, hy_conv, hy_w1, hy_b1, hy_w2, hy_b2, hy_w3, hy_b3,
           hy_freq, hy_decay, hy_skip, w_mo, w_ho, w_o, peer_wq, peer_keys, peer_u, peer_v, final_g):
    P = dict(zip(_LAYER_PARAMS, (ln1_g, ln2_g, w_in, m_conv, m_gate_b, m_norm_g, hy_conv, hy_w1, hy_b1,
                                 hy_w2, hy_b2, hy_w3, hy_b3, hy_freq, hy_decay, hy_skip, w_mo, w_ho, w_o,
                                 peer_wq, peer_keys, peer_u, peer_v)))
    B, S, D = x.shape
    depth = ln1_g.shape[0]
    tables = _dft_tables(S)
    x2 = x.reshape(B * S, D)
    (xn,) = _rmsnorm(x2, ln1_g[0], BF16)
    for l in range(depth):
        x2, yt = _layer(l, x2, xn, B, S, P, tables)
        if l + 1 < depth:
            x2, xn = _rmsnorm(x2, ln1_g[l + 1], BF16, add_t=yt, want_sum=True)
        else:
            (out,) = _rmsnorm(x2, final_g, F32, add_t=yt)
    return out.reshape(B, S, D)
```

```python
import functools
import math

import jax
import jax.numpy as jnp
from jax import lax
from jax.experimental import pallas as pl
from jax.experimental.pallas import tpu as pltpu

F32 = jnp.float32
BF16 = jnp.bfloat16

EPS = 1e-6
M_HEADS = 8
M_DQK = 256
M_DV = 512
M_CHUNK = 256
H_ORDER = 2
H_BANDS = 16
P_HEADS = 8
N_KEYS = 128
P_DKEY = 128
P_TOPK = 16

V7X_VMEM_BYTES = 64 * 1024 * 1024
VMEM_LIMIT = V7X_VMEM_BYTES - 8 * 1024 * 1024
LANES = 128
HI = lax.Precision.HIGHEST


def _cparams(*sem):
    return pltpu.CompilerParams(dimension_semantics=sem, vmem_limit_bytes=VMEM_LIMIT)


def _tile(dim, pref):
    t = min(dim, pref)
    while dim % t:
        t -= LANES
    assert t > 0 and dim % t == 0, (dim, pref)
    return t


def _rmsnorm_body(*refs, has_add, want_sum, want_t):
    x_ref, g_ref = refs[0], refs[1]
    pos = 2
    x = x_ref[...]
    if has_add:
        x = x + refs[pos][...].T
        pos += 1
    if want_sum:
        refs[pos][...] = x
        pos += 1
    ms = jnp.mean(x * x, axis=-1, keepdims=True)
    y = x * lax.rsqrt(ms + EPS) * g_ref[...]
    refs[pos][...] = y.astype(refs[pos].dtype)
    if want_t:
        refs[pos + 1][...] = y.T.astype(refs[pos + 1].dtype)


def _rmsnorm(x, g, out_dtype, add_t=None, want_sum=False, want_t=False):
    T, D = x.shape
    tm = _tile(T, 256)
    row = pl.BlockSpec((tm, D), lambda i: (i, 0))
    colT = pl.BlockSpec((D, tm), lambda i: (0, i))
    in_specs = [row, pl.BlockSpec((1, D), lambda i: (0, 0))]
    args = [x, g.reshape(1, D).astype(F32)]
    if add_t is not None:
        in_specs.append(colT)
        args.append(add_t)
    out_specs, out_shape = [], []
    if want_sum:
        out_specs.append(row)
        out_shape.append(jax.ShapeDtypeStruct((T, D), F32))
    out_specs.append(row)
    out_shape.append(jax.ShapeDtypeStruct((T, D), out_dtype))
    if want_t:
        out_specs.append(colT)
        out_shape.append(jax.ShapeDtypeStruct((D, T), out_dtype))
    return pl.pallas_call(
        functools.partial(_rmsnorm_body, has_add=add_t is not None, want_sum=want_sum, want_t=want_t),
        grid=(T // tm,),
        in_specs=in_specs, out_specs=out_specs, out_shape=out_shape,
        compiler_params=_cparams("parallel"),
        name="rmsnorm",
    )(*args)


def _mm_body(a_ref, b_ref, *rest, has_bias, has_res, cast_b):
    rest = list(rest)
    bq_ref = rest.pop() if cast_b else None
    o_ref = rest.pop()
    if cast_b:
        @pl.when(pl.program_id(1) == 0)
        def _():
            bq_ref[...] = b_ref[...].astype(BF16)
        b = bq_ref[...]
    else:
        b = b_ref[...]
    acc = jnp.dot(a_ref[...], b, preferred_element_type=F32)
    if has_bias:
        acc = acc + rest[0][...]
    if has_res:
        acc = acc + rest[1 if has_bias else 0][...]
    o_ref[...] = acc.astype(o_ref.dtype)


def _matmul(a, b, out_dtype, *, layer=None, n_cols=None, col_off=0, bias=None, residual=None,
            tm=512, tn=1024, name="matmul"):
    M, K = a.shape
    Nb = b.shape[-1]
    N = n_cols if n_cols is not None else Nb
    cast_b = b.dtype != BF16
    tm, tn = _tile(M, tm), _tile(N, tn)
    assert col_off % tn == 0
    co = col_off // tn
    n_outer = cast_b or a.size * (N // tn) + K * N < a.size + K * N * (M // tm)
    if n_outer:
        grid = (N // tn, M // tm)
        im = lambda j, i: (i, 0)
        jn = lambda j, i: (0, j)
        ij = lambda j, i: (i, j)
        jb2 = lambda j, i: (0, co + j)
        jb3 = lambda j, i: (layer, 0, co + j)
    else:
        grid = (M // tm, N // tn)
        im = lambda i, j: (i, 0)
        jn = lambda i, j: (0, j)
        ij = lambda i, j: (i, j)
        jb2 = lambda i, j: (0, co + j)
        jb3 = lambda i, j: (layer, 0, co + j)
    b_spec = pl.BlockSpec((K, tn), jb2) if b.ndim == 2 else pl.BlockSpec((None, K, tn), jb3)
    in_specs = [pl.BlockSpec((tm, K), im), b_spec]
    args = [a, b]
    if bias is not None:
        in_specs.append(pl.BlockSpec((1, tn), jn))
        args.append(bias.reshape(1, N).astype(F32))
    if residual is not None:
        in_specs.append(pl.BlockSpec((tm, tn), ij))
        args.append(residual)
    return pl.pallas_call(
        functools.partial(_mm_body, has_bias=bias is not None, has_res=residual is not None, cast_b=cast_b),
        grid=grid,
        in_specs=in_specs,
        out_specs=pl.BlockSpec((tm, tn), ij),
        out_shape=jax.ShapeDtypeStruct((M, N), out_dtype),
        scratch_shapes=[pltpu.VMEM((K, tn), BF16)] if cast_b else [],
        compiler_params=_cparams("parallel", "arbitrary" if cast_b else "parallel"),
        name=name,
    )(*args)


def _merge_body(a1_ref, b1_ref, a2_ref, b2_ref, g1_ref, g2_ref, o_ref):
    y1 = jnp.dot(a1_ref[...], b1_ref[...], preferred_element_type=F32)
    y2 = jnp.dot(a2_ref[...], b2_ref[...], preferred_element_type=F32)
    g1 = jax.nn.sigmoid(g1_ref[...].astype(F32))
    g2 = jax.nn.sigmoid(g2_ref[...].astype(F32))
    o_ref[...] = (g1 * y1 + g2 * y2).astype(o_ref.dtype)


def _merge_matmul(a1, b1, a2, b2, u, g1_off, g2_off, tm=512, tn=512):
    M, K = a1.shape
    _, N = b1.shape
    tm, tn = _tile(M, tm), _tile(N, tn)
    o1, o2 = g1_off // tn, g2_off // tn
    assert g1_off % tn == 0 and g2_off % tn == 0
    return pl.pallas_call(
        _merge_body,
        grid=(M // tm, N // tn),
        in_specs=[pl.BlockSpec((tm, K), lambda i, j: (i, 0)),
                  pl.BlockSpec((K, tn), lambda i, j: (0, j)),
                  pl.BlockSpec((tm, K), lambda i, j: (i, 0)),
                  pl.BlockSpec((K, tn), lambda i, j: (0, j)),
                  pl.BlockSpec((tm, tn), lambda i, j: (i, o1 + j)),
                  pl.BlockSpec((tm, tn), lambda i, j: (i, o2 + j))],
        out_specs=pl.BlockSpec((tm, tn), lambda i, j: (i, j)),
        out_shape=jax.ShapeDtypeStruct((M, N), BF16),
        compiler_params=_cparams("parallel", "parallel"),
        name="merge_matmul",
    )(a1, b1, a2, b2, u, u)


def _split_rows(scr_ref, y, store):
    half = y.shape[0] // 2
    for g in range(y.shape[1] // LANES):
        cols = slice(g * LANES, (g + 1) * LANES)
        scr_ref[g] = y[:, cols]
        store(cols, scr_ref[g, pl.ds(0, half, stride=2), :], scr_ref[g, pl.ds(1, half, stride=2), :])


def _dwconv_body(x_ref, w_ref, s_ref, o_ref, *scratch, silu, split):
    x = x_ref[0].astype(F32)
    S = x.shape[0]
    row = lax.broadcasted_iota(jnp.int32, x.shape, 0)
    xm = jnp.where(row == 0, 0.0, pltpu.roll(x, 1, axis=0))
    xp = jnp.where(row == S - 1, 0.0, pltpu.roll(x, S - 1, axis=0))
    w = w_ref[...]
    y = w[0:1, :] * xm + w[1:2, :] * x + w[2:3, :] * xp
    if silu:
        y = y * jax.nn.sigmoid(y)
    y = y * s_ref[...]
    if split:
        def store(cols, even, odd):
            o_ref[0, 0, :, cols] = even.astype(o_ref.dtype)
            o_ref[0, 1, :, cols] = odd.astype(o_ref.dtype)
        _split_rows(scratch[0], y, store)
    else:
        o_ref[0] = y.astype(o_ref.dtype)


def _dwconv(u3, col_off, w, scale, silu, split=False, tc=256):
    B, S, _ = u3.shape
    C = w.shape[1]
    tc = _tile(C, tc)
    assert col_off % tc == 0
    o = col_off // tc
    if split:
        out_spec = pl.BlockSpec((1, 2, S // 2, tc), lambda b, j: (b, 0, 0, j))
        out_shape = jax.ShapeDtypeStruct((B, 2, S // 2, C), BF16)
    else:
        out_spec = pl.BlockSpec((1, S, tc), lambda b, j: (b, 0, j))
        out_shape = jax.ShapeDtypeStruct((B, S, C), BF16)
    return pl.pallas_call(
        functools.partial(_dwconv_body, silu=silu, split=split),
        grid=(B, C // tc),
        in_specs=[pl.BlockSpec((1, S, tc), lambda b, j: (b, 0, o + j)),
                  pl.BlockSpec((3, tc), lambda b, j: (0, j)),
                  pl.BlockSpec((1, tc), lambda b, j: (0, j))],
        out_specs=out_spec, out_shape=out_shape,
        scratch_shapes=[pltpu.VMEM((tc // LANES, S, LANES), F32)] if split else [],
        compiler_params=_cparams("parallel", "parallel"),
        name="dwconv_silu" if silu else "dwconv",
    )(u3, w.astype(F32), scale.reshape(1, C).astype(F32))


def _log_sigmoid(x):
    return jnp.minimum(x, 0.0) - jnp.log1p(jnp.exp(-jnp.abs(x)))


def _mlstm_body(q_ref, k_ref, v_ref, o_ref, grow_ref, gcol_ref, ng_ref, out_ref,
                c_ref, n_ref, m_ref, h_ref, *, L):
    S = q_ref.shape[1]
    nc = S // L
    row = lax.broadcasted_iota(jnp.int32, (L, L), 0)
    col = lax.broadcasted_iota(jnp.int32, (L, L), 1)

    c_ref[...] = jnp.zeros_like(c_ref)
    n_ref[...] = jnp.zeros_like(n_ref)
    m_ref[...] = jnp.zeros_like(m_ref)

    def direction(rev):
        d = 1 if rev else 0
        gi, gf = (2, 3) if rev else (0, 1)
        keep = (col >= row) if rev else (col <= row)
        cum_rs = ((row >= col) if rev else (row <= col)).astype(F32)
        cum_tr = keep.astype(F32)
        last = 0 if rev else L - 1

        def chunk(cc, partner_done):
            r0 = pl.multiple_of(cc * L, L)
            q = q_ref[0, pl.ds(r0, L), :]
            k = k_ref[0, pl.ds(r0, L), :]
            v = v_ref[0, pl.ds(r0, L), :]
            grow = grow_ref[0, 0, :, pl.ds(r0, L)]
            gcol = gcol_ref[0, 0, pl.ds(r0, L), :]
            b_rows = jnp.dot(_log_sigmoid(grow), cum_rs, precision=HI, preferred_element_type=F32)
            b_cols = jnp.dot(cum_tr, _log_sigmoid(gcol), precision=HI, preferred_element_type=F32)
            b_r = b_rows[gf:gf + 1, :]
            b_c = b_cols[:, gf:gf + 1]
            li_r = grow[gi:gi + 1, :]
            li_c = gcol[:, gi:gi + 1]
            m_prev = m_ref[d]

            dlog = jnp.where(keep, b_c - b_r + li_r, -jnp.inf)
            inter = b_c + m_prev
            m_t = jnp.maximum(inter, jnp.max(dlog, axis=1, keepdims=True))
            w_intra = jnp.exp(dlog - m_t)
            w_inter = jnp.exp(inter - m_t)
            s = lax.dot_general(q, k, (((1,), (1,)), ((), ())), preferred_element_type=F32) * w_intra
            cmat = c_ref[d]
            num = w_inter * jnp.dot(q, cmat.astype(BF16), preferred_element_type=F32) \
                + jnp.dot(s.astype(BF16), v, preferred_element_type=F32)
            qn = jnp.sum(q.astype(F32) * n_ref[d], axis=1, keepdims=True)
            den = w_inter * qn + jnp.sum(s, axis=1, keepdims=True)
            h = num / jnp.maximum(jnp.abs(den), jnp.exp(-m_t))

            b_last = b_r[:, last:last + 1]
            gl = b_last - b_c + li_c
            m_new = jnp.maximum(b_last + m_prev, jnp.max(gl, axis=0, keepdims=True))
            dec = jnp.exp(b_last + m_prev - m_new)
            kw = k.astype(F32) * jnp.exp(gl - m_new)
            c_ref[d] = dec * cmat + lax.dot_general(
                kw.astype(BF16), v, (((0,), (0,)), ((), ())), preferred_element_type=F32)
            n_ref[d] = dec * n_ref[d] + jnp.sum(kw, axis=0, keepdims=True)
            m_ref[d] = m_new

            if not partner_done:
                h_ref[pl.ds(r0, L), :] = h
            else:
                hs = h_ref[pl.ds(r0, L), :] + h
                hn = hs * lax.rsqrt(jnp.mean(hs * hs, axis=1, keepdims=True) + EPS)
                gate = jax.nn.sigmoid(o_ref[0, pl.ds(r0, L), :].astype(F32))
                out_ref[0, pl.ds(r0, L), :] = (hn * ng_ref[...] * gate).astype(out_ref.dtype)

        return chunk

    fwd, bwd = direction(False), direction(True)

    def step(c, partner_done):
        fwd(c, partner_done)
        bwd(nc - 1 - c, partner_done)

    def first_half(c, carry):
        step(c, False)
        return carry

    def second_half(c, carry):
        step(c, True)
        return carry

    lax.fori_loop(0, nc // 2, first_half, 0)
    lax.fori_loop(nc // 2, nc, second_half, 0)


def _mlstm(qk, u3, v_off, o_off, grow, gcol, norm_g):
    B, S, _ = qk.shape
    H, dk, dv = M_HEADS, M_DQK, M_DV
    L = _tile(S, M_CHUNK)
    vo, oo = v_off // dv, o_off // dv
    assert v_off % dv == 0 and o_off % dv == 0
    assert (S // L) % 2 == 0, "the two directions are paired chunk by chunk"
    return pl.pallas_call(
        functools.partial(_mlstm_body, L=L),
        grid=(B, H),
        in_specs=[pl.BlockSpec((1, S, dk), lambda b, h: (b, 0, h)),
                  pl.BlockSpec((1, S, dk), lambda b, h: (b, 0, H + h)),
                  pl.BlockSpec((1, S, dv), lambda b, h: (b, 0, vo + h)),
                  pl.BlockSpec((1, S, dv), lambda b, h: (b, 0, oo + h)),
                  pl.BlockSpec((1, 1, 8, S), lambda b, h: (b, h, 0, 0)),
                  pl.BlockSpec((1, 1, S, 8), lambda b, h: (b, h, 0, 0)),
                  pl.BlockSpec((1, dv), lambda b, h: (0, h))],
        out_specs=pl.BlockSpec((1, S, dv), lambda b, h: (b, 0, h)),
        out_shape=jax.ShapeDtypeStruct((B, S, H * dv), BF16),
        scratch_shapes=[pltpu.VMEM((2, dk, dv), F32), pltpu.VMEM((2, 1, dk), F32),
                        pltpu.VMEM((2, 1, 1), F32), pltpu.VMEM((S, dv), F32)],
        compiler_params=_cparams("parallel", "parallel"),
        name="mlstm",
    )(qk, qk, u3, u3, grow, gcol, norm_g.reshape(1, H * dv).astype(F32))


def _hy_mlp_body(z_ref, w1_ref, b1_ref, w2_ref, b2_ref, fr_ref, o_ref):
    fr = fr_ref[...]
    h = jnp.sin(fr * (jnp.dot(z_ref[...], w1_ref[...], precision=HI, preferred_element_type=F32) + b1_ref[...]))
    h = jnp.sin(fr * (jnp.dot(h, w2_ref[...], precision=HI, preferred_element_type=F32) + b2_ref[...]))
    o_ref[...] = h


def _hy_filter_body(h_ref, wf_ref, wb_ref, bf_ref, bb_ref, df_ref, db_ref, hs_ref, hd_ref, scr_ref):
    L = h_ref.shape[0]
    h2 = h_ref[...]
    tc = wf_ref.shape[1]
    row = lax.broadcasted_iota(jnp.int32, (L, tc), 0)
    tn = row.astype(F32) / L
    hf = (jnp.dot(h2, wf_ref[...], precision=HI, preferred_element_type=F32) + bf_ref[...]) \
        * jnp.exp(-tn * jnp.abs(df_ref[...]))
    hb = (jnp.dot(h2, wb_ref[...], precision=HI, preferred_element_type=F32) + bb_ref[...]) \
        * jnp.exp(-tn * jnp.abs(db_ref[...]))
    hb = jnp.where(row == 0, 0.0, hb)
    norm = jnp.sum(jnp.abs(hf), axis=0, keepdims=True) + jnp.sum(jnp.abs(hb), axis=0, keepdims=True)
    inv = 1.0 / norm
    for o_ref, val in ((hs_ref, (hf + hb) * inv), (hd_ref, (hb - hf) * inv)):
        def store(cols, even, odd, o_ref=o_ref):
            o_ref[0, :, cols] = even.astype(o_ref.dtype)
            o_ref[1, :, cols] = odd.astype(o_ref.dtype)
        _split_rows(scr_ref, val, store)


def _hy_filters(L, w1, b1, w2, b2, w3, b3, freq, decay, tc=256):
    hid = w1.shape[1]
    C = w3.shape[1] // (2 * H_ORDER)
    tn = jnp.arange(L, dtype=F32) / L
    bands = jnp.linspace(1e-4, H_BANDS - 1, H_BANDS, dtype=F32)
    ang = (2.0 * math.pi) * tn[:, None] * bands[None, :]
    z = jnp.concatenate([tn[:, None], jnp.cos(ang), jnp.sin(ang)], axis=-1)
    emb = z.shape[1]
    z = jnp.pad(z, ((0, 0), (0, LANES - emb)))
    w1p = jnp.pad(w1.astype(F32), ((0, LANES - emb), (0, 0)))
    row = lambda a: a.reshape(1, -1).astype(F32)
    h2 = pl.pallas_call(
        _hy_mlp_body,
        out_shape=jax.ShapeDtypeStruct((L, hid), F32),
        compiler_params=pltpu.CompilerParams(vmem_limit_bytes=VMEM_LIMIT),
        name="hyena_mlp",
    )(z, w1p, row(b1), w2.astype(F32), row(b2), row(freq))
    tc = _tile(C, tc)
    nct = C // tc
    fwd = lambda o, j: (0, o * nct + j)
    bwd = lambda o, j: (0, (H_ORDER + o) * nct + j)
    out = pl.BlockSpec((2, L // 2, tc), lambda o, j: (0, 0, o * nct + j))
    w3 = w3.astype(F32)
    b3r, dcr = row(b3), row(decay)
    return pl.pallas_call(
        _hy_filter_body,
        grid=(H_ORDER, nct),
        in_specs=[pl.BlockSpec((L, hid), lambda o, j: (0, 0)),
                  pl.BlockSpec((hid, tc), fwd), pl.BlockSpec((hid, tc), bwd),
                  pl.BlockSpec((1, tc), fwd), pl.BlockSpec((1, tc), bwd),
                  pl.BlockSpec((1, tc), fwd), pl.BlockSpec((1, tc), bwd)],
        out_specs=[out, out],
        out_shape=[jax.ShapeDtypeStruct((2, L // 2, H_ORDER * C), BF16)] * 2,
        scratch_shapes=[pltpu.VMEM((tc // LANES, L, LANES), F32)],
        compiler_params=_cparams("parallel", "parallel"),
        name="hyena_filter",
    )(h2, w3, w3, b3r, b3r, dcr, dcr)


def _dft_tables(L):
    L2 = L // 2
    kap = jnp.arange(L2, dtype=jnp.int32)
    m = ((2 * kap[:, None] + 1) * kap[None, :]) % (2 * L)
    ang = m.astype(F32) * F32(math.pi / L)
    gc, gs = jnp.cos(ang), jnp.sin(ang)
    w_lo = (2 * kap + 1).astype(F32) * F32(math.pi / (2 * L))
    w_hi = (2 * (L - 1 - kap) + 1).astype(F32) * F32(math.pi / (2 * L))
    tw = jnp.stack([jnp.cos(w_lo), jnp.sin(w_lo), jnp.cos(w_hi), jnp.sin(w_hi)], axis=1)
    return gc.astype(BF16), gs.astype(BF16), gc.T.astype(BF16), gs.T.astype(BF16), tw


def _twiddles(tw_ref):
    tw = tw_ref[...]
    return tw[:, 0:1], tw[:, 1:2], tw[:, 2:3], tw[:, 3:4]


def _hy_kspec_body(gc_ref, gs_ref, ae_ref, ao_ref, be_ref, bo_ref, tw_ref, kre_ref, kim_ref):
    gc, gs = gc_ref[...], gs_ref[...]
    c, s, ch, sh = _twiddles(tw_ref)
    dot = functools.partial(jnp.dot, preferred_element_type=F32)
    p1, p2, p3 = dot(gc, ae_ref[...]), dot(gc, ao_ref[...]), dot(gs, ao_ref[...])
    p4, p5, p6 = dot(gs, be_ref[...]), dot(gc, bo_ref[...]), dot(gs, bo_ref[...])
    kre_ref[0] = p1 + c * p2 - s * p3
    kim_ref[0] = p4 + s * p5 + c * p6
    kre_ref[1] = p1 + ch * p2 + sh * p3
    kim_ref[1] = -p4 + sh * p5 - ch * p6


def _hy_kspec(gc, gs, tw, hsum, hdiff, tm=512, tn=256):
    _, L2, OC = hsum.shape
    a2, b2 = hsum, hdiff
    tm, tn = _tile(L2, tm), _tile(OC, tn)
    nct = OC // tn
    tab = pl.BlockSpec((tm, L2), lambda j, i: (i, 0))
    even = pl.BlockSpec((None, L2, tn), lambda j, i: (0, 0, j))
    odd = pl.BlockSpec((None, L2, tn), lambda j, i: (1, 0, j))
    out = pl.BlockSpec((2, tm, tn), lambda j, i: (0, i, j))
    shp = jax.ShapeDtypeStruct((2, L2, OC), F32)
    return pl.pallas_call(
        _hy_kspec_body,
        grid=(nct, L2 // tm),
        in_specs=[tab, tab, even, odd, even, odd, pl.BlockSpec((tm, 4), lambda j, i: (i, 0))],
        out_specs=[out, out], out_shape=[shp, shp],
        compiler_params=_cparams("parallel", "parallel"),
        name="hyena_kspec",
    )(gc, gs, a2, a2, b2, b2, tw)


def _hy_fwd_body(gc_ref, gs_ref, ze_ref, zo_ref, kre_ref, kim_ref, tw_ref, wre_ref, wim_ref):
    gc, gs = gc_ref[...], gs_ref[...]
    c, s, ch, sh = _twiddles(tw_ref)
    kre_l, kim_l, kre_h, kim_h = kre_ref[0], kim_ref[0], kre_ref[1], kim_ref[1]
    dot = functools.partial(jnp.dot, preferred_element_type=F32)
    for b in range(ze_ref.shape[0]):
        r0, s0 = dot(gc, ze_ref[b]), dot(gs, ze_ref[b])
        r1, s1 = dot(gc, zo_ref[b]), dot(gs, zo_ref[b])
        zre_l = r0 + c * r1 - s * s1
        zim_l = -s0 - c * s1 - s * r1
        zre_h = r0 + ch * r1 + sh * s1
        zim_h = s0 + ch * s1 - sh * r1
        a = zre_l * kre_l - zim_l * kim_l
        bb = zre_l * kim_l + zim_l * kre_l
        p = zre_h * kre_h - zim_h * kim_h
        q = zre_h * kim_h + zim_h * kre_h
        wre_ref[b, 0] = (a + p).astype(wre_ref.dtype)
        wim_ref[b, 0] = (bb - q).astype(wim_ref.dtype)
        wre_ref[b, 1] = (a * c - bb * s + p * ch - q * sh).astype(wre_ref.dtype)
        wim_ref[b, 1] = (a * s + bb * c - p * sh - q * ch).astype(wim_ref.dtype)


def _hy_fwd(gc, gs, tw, z4, z_off, kre, kim, order, C, tm=512, tn=256):
    B, _, L2, _ = z4.shape
    tm, tn = _tile(L2, tm), _tile(C, tn)
    nct = C // tn
    zo = z_off // tn
    assert z_off % tn == 0
    tab = pl.BlockSpec((tm, L2), lambda j, i: (i, 0))
    ksp = pl.BlockSpec((2, tm, tn), lambda j, i: (0, i, order * nct + j))
    out = pl.BlockSpec((B, 2, tm, tn), lambda j, i: (0, 0, i, j))
    shp = jax.ShapeDtypeStruct((B, 2, L2, C), BF16)
    return pl.pallas_call(
        _hy_fwd_body,
        grid=(nct, L2 // tm),
        in_specs=[tab, tab,
                  pl.BlockSpec((B, None, L2, tn), lambda j, i: (0, 0, 0, zo + j)),
                  pl.BlockSpec((B, None, L2, tn), lambda j, i: (0, 1, 0, zo + j)),
                  ksp, ksp, pl.BlockSpec((tm, 4), lambda j, i: (i, 0))],
        out_specs=[out, out], out_shape=[shp, shp],
        compiler_params=_cparams("parallel", "parallel"),
        name="hyena_fwd",
    )(gc, gs, z4, z4, kre, kim, tw)


def _hy_inv_body(gct_ref, gst_ref, wre_ref, wim_ref, z_ref, g_ref, skip_ref, o_ref, *scratch, scale, interleave):
    gct, gst = gct_ref[...], gst_ref[...]
    skip = skip_ref[...]
    tm = gct.shape[0]
    dot = functools.partial(jnp.dot, preferred_element_type=F32)
    for b in range(z_ref.shape[0]):
        for j in range(2):
            y = dot(gct, wre_ref[b, j]) - dot(gst, wim_ref[b, j])
            out = g_ref[b, j].astype(F32) * (scale * y + skip * z_ref[b, j].astype(F32))
            if interleave:
                for g in range(out.shape[1] // LANES):
                    scratch[0][g, pl.ds(j, tm, stride=2), :] = out[:, g * LANES:(g + 1) * LANES]
            else:
                o_ref[b, j] = out.astype(o_ref.dtype)
        if interleave:
            for g in range(o_ref.shape[2] // LANES):
                o_ref[b, :, g * LANES:(g + 1) * LANES] = scratch[0][g].astype(o_ref.dtype)


def _hy_inv(gct, gst, wre, wim, z4, z_off, g4, g_off, skip, interleave, tm=512, tn=256):
    B, _, L2, C = wre.shape
    tm, tn = _tile(L2, tm), _tile(C, tn)
    zo, go = z_off // tn, g_off // tn
    assert z_off % tn == 0 and g_off % tn == 0
    tab = pl.BlockSpec((tm, L2), lambda j, i: (i, 0))
    wsp = pl.BlockSpec((B, 2, L2, tn), lambda j, i: (0, 0, 0, j))
    rows = lambda off: pl.BlockSpec((B, 2, tm, tn), lambda j, i: (0, 0, i, off + j))
    if interleave:
        out_spec = pl.BlockSpec((B, 2 * tm, tn), lambda j, i: (0, i, j))
        out_shape = jax.ShapeDtypeStruct((B, 2 * L2, C), BF16)
    else:
        out_spec, out_shape = rows(0), jax.ShapeDtypeStruct((B, 2, L2, C), BF16)
    return pl.pallas_call(
        functools.partial(_hy_inv_body, scale=1.0 / (2 * L2), interleave=interleave),
        grid=(C // tn, L2 // tm),
        in_specs=[tab, tab, wsp, wsp, rows(zo), rows(go), pl.BlockSpec((1, tn), lambda j, i: (0, j))],
        out_specs=out_spec, out_shape=out_shape,
        scratch_shapes=[pltpu.VMEM((tn // LANES, 2 * tm, LANES), F32)] if interleave else [],
        compiler_params=_cparams("parallel", "parallel"),
        name="hyena_inv",
    )(gct, gst, wre, wim, z4, g4, skip.reshape(1, C).astype(F32))


def _top_values(x, n):
    rank = lax.broadcasted_iota(jnp.int32, (n, x.shape[1]), 0)
    vals = jnp.zeros((n, x.shape[1]), F32)
    work = x
    for r in range(n):
        mx = jnp.max(work, axis=0, keepdims=True)
        vals = jnp.where(rank == r, mx, vals)
        work = jnp.where(work >= mx, -jnp.inf, work)
    return vals, mx


def _peer_route_body(q_ref, k_ref, s1_ref, e1_ref, th_ref, e0_ref):
    nt = (((1,), (1,)), ((), ()))
    s0 = lax.dot_general(k_ref[0], q_ref[0], nt, preferred_element_type=F32)
    s1 = lax.dot_general(k_ref[1], q_ref[1], nt, preferred_element_type=F32)
    v0, _ = _top_values(s0, P_TOPK)
    v1, _ = _top_values(s1, P_TOPK)
    cand = jnp.concatenate([v0[a:a + 1, :] + v1 for a in range(P_TOPK)], axis=0)
    _, tau = _top_values(cand, P_TOPK)
    top = v0[0:1, :] + v1[0:1, :]
    zsum = jnp.sum(jnp.where(cand >= tau, jnp.exp(cand - top), 0.0), axis=0, keepdims=True)
    th = jnp.full(s0.shape, jnp.inf, F32)
    for b in range(P_TOPK):
        vb = v1[b:b + 1, :]
        th = jnp.where(s0 + vb >= tau, vb, th)
    s1_ref[0] = s1
    e1_ref[0] = jnp.exp(s1 - v1[0:1, :])
    th_ref[0] = th
    e0_ref[0] = jnp.exp(s0 - v0[0:1, :]) / zsum


def _peer_route(q3, keys, tm=256):
    H2, T, dk = q3.shape
    H = H2 // 2
    tm = _tile(T, tm)
    big = pl.BlockSpec((1, N_KEYS, tm), lambda i, h: (h, 0, i))
    shp = jax.ShapeDtypeStruct((H, N_KEYS, T), F32)
    return pl.pallas_call(
        _peer_route_body,
        grid=(T // tm, H),
        in_specs=[pl.BlockSpec((2, tm, dk), lambda i, h: (h, i, 0)),
                  pl.BlockSpec((2, N_KEYS, dk), lambda i, h: (h, 0, 0))],
        out_specs=[big] * 4, out_shape=[shp] * 4,
        compiler_params=_cparams("parallel", "parallel"),
        name="peer_route",
    )(q3, keys)


GATE_ROWS = 32


def _peer_expert_body(xt_ref, u_ref, vt_ref, s1_ref, e1_ref, th_ref, e0_ref, o_ref, a_ref, w_ref, *, te):
    e = pl.program_id(1)
    tm = a_ref.shape[1]

    @pl.when(e == 0)
    def _():
        o_ref[...] = jnp.zeros_like(o_ref)

    a_ref[...] = jnp.dot(u_ref[...], xt_ref[...], preferred_element_type=F32)
    for ii in range(te // N_KEYS):
        for tc in range(tm // LANES):
            cols = slice(tc * LANES, (tc + 1) * LANES)
            th_rows = [th_ref[h, ii:ii + 1, cols] for h in range(P_HEADS)]
            e0_rows = [e0_ref[h, ii:ii + 1, cols] for h in range(P_HEADS)]
            for jb in range(N_KEYS // GATE_ROWS):
                rows = slice(jb * GATE_ROWS, (jb + 1) * GATE_ROWS)
                g = None
                for h in range(P_HEADS):
                    term = e0_rows[h] * jnp.where(s1_ref[h, rows, cols] >= th_rows[h], e1_ref[h, rows, cols], 0.0)
                    g = term if g is None else g + term
                erows = slice(ii * N_KEYS + jb * GATE_ROWS, ii * N_KEYS + (jb + 1) * GATE_ROWS)
                a = a_ref[erows, cols]
                act = 0.5 * a * (1.0 + lax.erf(a * (1.0 / math.sqrt(2.0))))
                w_ref[erows, cols] = (g * act).astype(BF16)
    o_ref[...] += jnp.dot(vt_ref[...], w_ref[...], preferred_element_type=F32)


def _peer_expert(xt, U, Vt, s1, e1, th, e0, tm=512, te=512):
    D, T = xt.shape
    E = U.shape[0]
    tm, te = _tile(T, tm), _tile(E, te)
    H = s1.shape[0]
    n_i = te // N_KEYS
    sub = 8
    assert n_i <= sub
    by_tile = lambda a: jnp.pad(a.reshape(H, E // te, n_i, T), ((0, 0), (0, 0), (0, sub - n_i), (0, 0)))
    once = dict(pipeline_mode=pl.Buffered(1))
    plane = pl.BlockSpec((H, N_KEYS, tm), lambda i, e: (0, 0, i), **once)
    rows = pl.BlockSpec((H, None, sub, tm), lambda i, e: (0, e, 0, i))
    return pl.pallas_call(
        functools.partial(_peer_expert_body, te=te),
        grid=(T // tm, E // te),
        in_specs=[pl.BlockSpec((D, tm), lambda i, e: (0, i), **once),
                  pl.BlockSpec((te, D), lambda i, e: (e, 0)),
                  pl.BlockSpec((D, te), lambda i, e: (0, e)),
                  plane, plane, rows, rows],
        out_specs=pl.BlockSpec((D, tm), lambda i, e: (0, i)),
        out_shape=jax.ShapeDtypeStruct((D, T), F32),
        scratch_shapes=[pltpu.VMEM((te, tm), F32), pltpu.VMEM((te, tm), BF16)],
        compiler_params=_cparams("parallel", "arbitrary"),
        name="peer_expert",
    )(xt, U, Vt, s1, e1, by_tile(th), by_tile(e0))


def _layer(l, x2, xn, B, S, P, tables):
    T, D = x2.shape
    H, dk, dv = M_HEADS, M_DQK, M_DV
    mqk, mv = H * dk, H * dv
    n_gate = 4 * H
    C = P["hy_skip"].shape[2]
    m_cols = 2 * mqk + 2 * mv + n_gate
    assert (m_cols - n_gate) % LANES == 0
    w_in = P["w_in"][l]

    u_m = _matmul(xn, w_in[:, :m_cols - n_gate].astype(BF16), BF16, name="in_proj_mlstm")
    w_gate = jnp.pad(w_in[:, m_cols - n_gate:m_cols], ((0, 0), (0, LANES - n_gate))).astype(BF16)
    b_gate = jnp.pad(P["m_gate_b"][l], (0, LANES - n_gate))
    gates = _matmul(xn, w_gate, F32, bias=b_gate, name="gate_proj")
    u_hg = _matmul(xn, w_in[:, m_cols:].astype(BF16), BF16, name="in_proj_hyena")
    um3 = u_m.reshape(B, S, u_m.shape[1])

    qk_scale = jnp.concatenate([jnp.ones((mqk,), F32), jnp.full((mqk,), dk ** -0.5, F32)])
    qk = _dwconv(um3, 0, P["m_conv"][l], qk_scale, silu=True)
    g5 = gates[:, :n_gate].reshape(B, S, 4, H)
    grow = jnp.pad(jnp.transpose(g5, (0, 3, 2, 1)), ((0, 0), (0, 0), (0, 4), (0, 0)))
    gcol = jnp.pad(jnp.transpose(g5, (0, 3, 1, 2)), ((0, 0), (0, 0), (0, 0), (0, 4)))
    hm = _mlstm(qk, um3, 2 * mqk, 2 * mqk + mv, grow, gcol, P["m_norm_g"][l])

    gc, gs, gct, gst, tw = tables
    nh = (H_ORDER + 1) * C
    hy4 = _dwconv(u_hg.reshape(B, S, u_hg.shape[1]), 0, P["hy_conv"][l], jnp.ones((nh,), F32), silu=False,
                  split=True)
    hsum, hdiff = _hy_filters(S, P["hy_w1"][l], P["hy_b1"][l], P["hy_w2"][l], P["hy_b2"][l], P["hy_w3"][l],
                              P["hy_b3"][l], P["hy_freq"][l], P["hy_decay"][l])
    kre, kim = _hy_kspec(gc, gs, tw, hsum, hdiff)
    z4, z_off = hy4, 0
    for o in range(H_ORDER):
        wre, wim = _hy_fwd(gc, gs, tw, z4, z_off, kre, kim, o, C)
        z4 = _hy_inv(gct, gst, wre, wim, z4, z_off, hy4, (o + 1) * C, P["hy_skip"][l, o],
                     interleave=o == H_ORDER - 1)
        z_off = 0
    hy = z4.reshape(T, C)

    merged = _merge_matmul(hm.reshape(T, mv), P["w_mo"][l].astype(BF16), hy, P["w_ho"][l].astype(BF16),
                           u_hg, nh, nh + D)
    x2 = _matmul(merged, P["w_o"], F32, layer=l, residual=x2, tn=512, name="out_proj")

    xn2, xn2t = _rmsnorm(x2, P["ln2_g"][l], BF16, want_t=True)
    q = _matmul(xn2, P["peer_wq"], BF16, layer=l, tn=512, name="peer_query")
    q3 = jnp.transpose(q.reshape(T, 2 * P_HEADS, P_DKEY), (1, 0, 2))
    keys = P["peer_keys"][l].reshape(2 * P_HEADS, N_KEYS, P_DKEY).astype(BF16)
    s1, e1, th, e0 = _peer_route(q3, keys)
    yt = _peer_expert(xn2t, P["peer_u"][l].astype(BF16), P["peer_v"][l].T.astype(BF16), s1, e1, th, e0)
    return x2, yt


_LAYER_PARAMS = ("ln1_g", "ln2_g", "w_in", "m_conv", "m_gate_b", "m_norm_g", "hy_conv", "hy_w1", "hy_b1",
                 "hy_w2", "hy_b2", "hy_w3", "hy_b3", "hy_freq", "hy_decay", "hy_skip", "w_mo", "w_ho", "w_o",
                 "peer_wq", "peer_keys", "peer_u", "peer_v")


def kernel(x, ln1_g, ln2_g, w_in, m_conv, m_gate_b, m_norm_g, hy_conv, hy_w1, hy_b1, hy_w2, hy_b2, hy_w3, hy_b3,
           hy_freq, hy_decay, hy_skip, w_mo, w_ho, w_o, peer_wq, peer_keys, peer_u, peer_v, final_g):
    P = dict(zip(_LAYER_PARAMS, (ln1_g, ln2_g, w_in, m_conv, m_gate_b, m_norm_g, hy_conv, hy_w1, hy_b1,
                                 hy_w2, hy_b2, hy_w3, hy_b3, hy_freq, hy_decay, hy_skip, w_mo, w_ho, w_o,
                                 peer_wq, peer_keys, peer_u, peer_v)))
    B, S, D = x.shape
    depth = ln1_g.shape[0]
    tables = _dft_tables(S)
    x2 = x.reshape(B * S, D)
    (xn,) = _rmsnorm(x2, ln1_g[0], BF16)
    for l in range(depth):
        x2, yt = _layer(l, x2, xn, B, S, P, tables)
        if l + 1 < depth:
            x2, xn = _rmsnorm(x2, ln1_g[l + 1], BF16, add_t=yt, want_sum=True)
        else:
            (out,) = _rmsnorm(x2, final_g, F32, add_t=yt)
    return out.reshape(B, S, D)
```

```python
import functools
import math

import jax
import jax.numpy as jnp
from jax import lax
from jax.experimental import pallas as pl
from jax.experimental.pallas import tpu as pltpu

F32 = jnp.float32
BF16 = jnp.bfloat16

EPS = 1e-6
M_HEADS = 8
M_DQK = 256
M_DV = 512
M_CHUNK = 256
H_ORDER = 2
H_BANDS = 16
P_HEADS = 8
N_KEYS = 128
P_DKEY = 128
P_TOPK = 16

V7X_VMEM_BYTES = 64 * 1024 * 1024
VMEM_LIMIT = V7X_VMEM_BYTES - 8 * 1024 * 1024
LANES = 128
HI = lax.Precision.HIGHEST


def _cparams(*sem):
    return pltpu.CompilerParams(dimension_semantics=sem, vmem_limit_bytes=VMEM_LIMIT)


def _tile(dim, pref):
    t = min(dim, pref)
    while dim % t:
        t -= LANES
    assert t > 0 and dim % t == 0, (dim, pref)
    return t


def _rmsnorm_body(*refs, has_add, want_sum, want_t):
    x_ref, g_ref = refs[0], refs[1]
    pos = 2
    x = x_ref[...]
    if has_add:
        x = x + refs[pos][...].T
        pos += 1
    if want_sum:
        refs[pos][...] = x
        pos += 1
    ms = jnp.mean(x * x, axis=-1, keepdims=True)
    y = x * lax.rsqrt(ms + EPS) * g_ref[...]
    refs[pos][...] = y.astype(refs[pos].dtype)
    if want_t:
        refs[pos + 1][...] = y.T.astype(refs[pos + 1].dtype)


def _rmsnorm(x, g, out_dtype, add_t=None, want_sum=False, want_t=False):
    T, D = x.shape
    tm = _tile(T, 256)
    row = pl.BlockSpec((tm, D), lambda i: (i, 0))
    colT = pl.BlockSpec((D, tm), lambda i: (0, i))
    in_specs = [row, pl.BlockSpec((1, D), lambda i: (0, 0))]
    args = [x, g.reshape(1, D).astype(F32)]
    if add_t is not None:
        in_specs.append(colT)
        args.append(add_t)
    out_specs, out_shape = [], []
    if want_sum:
        out_specs.append(row)
        out_shape.append(jax.ShapeDtypeStruct((T, D), F32))
    out_specs.append(row)
    out_shape.append(jax.ShapeDtypeStruct((T, D), out_dtype))
    if want_t:
        out_specs.append(colT)
        out_shape.append(jax.ShapeDtypeStruct((D, T), out_dtype))
    return pl.pallas_call(
        functools.partial(_rmsnorm_body, has_add=add_t is not None, want_sum=want_sum, want_t=want_t),
        grid=(T // tm,),
        in_specs=in_specs, out_specs=out_specs, out_shape=out_shape,
        compiler_params=_cparams("parallel"),
        name="rmsnorm",
    )(*args)


def _mm_body(a_ref, b_ref, *rest, has_bias, has_res, cast_b):
    rest = list(rest)
    bq_ref = rest.pop() if cast_b else None
    o_ref = rest.pop()
    if cast_b:
        @pl.when(pl.program_id(1) == 0)
        def _():
            bq_ref[...] = b_ref[...].astype(BF16)
        b = bq_ref[...]
    else:
        b = b_ref[...]
    acc = jnp.dot(a_ref[...], b, preferred_element_type=F32)
    if has_bias:
        acc = acc + rest[0][...]
    if has_res:
        acc = acc + rest[1 if has_bias else 0][...]
    o_ref[...] = acc.astype(o_ref.dtype)


def _matmul(a, b, out_dtype, *, layer=None, n_cols=None, col_off=0, bias=None, residual=None,
            tm=512, tn=1024, name="matmul"):
    M, K = a.shape
    Nb = b.shape[-1]
    N = n_cols if n_cols is not None else Nb
    cast_b = b.dtype != BF16
    tm, tn = _tile(M, tm), _tile(N, tn)
    assert col_off % tn == 0
    co = col_off // tn
    n_outer = cast_b or a.size * (N // tn) + K * N < a.size + K * N * (M // tm)
    if n_outer:
        grid = (N // tn, M // tm)
        im = lambda j, i: (i, 0)
        jn = lambda j, i: (0, j)
        ij = lambda j, i: (i, j)
        jb2 = lambda j, i: (0, co + j)
        jb3 = lambda j, i: (layer, 0, co + j)
    else:
        grid = (M // tm, N // tn)
        im = lambda i, j: (i, 0)
        jn = lambda i, j: (0, j)
        ij = lambda i, j: (i, j)
        jb2 = lambda i, j: (0, co + j)
        jb3 = lambda i, j: (layer, 0, co + j)
    b_spec = pl.BlockSpec((K, tn), jb2) if b.ndim == 2 else pl.BlockSpec((None, K, tn), jb3)
    in_specs = [pl.BlockSpec((tm, K), im), b_spec]
    args = [a, b]
    if bias is not None:
        in_specs.append(pl.BlockSpec((1, tn), jn))
        args.append(bias.reshape(1, N).astype(F32))
    if residual is not None:
        in_specs.append(pl.BlockSpec((tm, tn), ij))
        args.append(residual)
    return pl.pallas_call(
        functools.partial(_mm_body, has_bias=bias is not None, has_res=residual is not None, cast_b=cast_b),
        grid=grid,
        in_specs=in_specs,
        out_specs=pl.BlockSpec((tm, tn), ij),
        out_shape=jax.ShapeDtypeStruct((M, N), out_dtype),
        scratch_shapes=[pltpu.VMEM((K, tn), BF16)] if cast_b else [],
        compiler_params=_cparams("parallel", "arbitrary" if cast_b else "parallel"),
        name=name,
    )(*args)


def _merge_body(a1_ref, b1_ref, a2_ref, b2_ref, g1_ref, g2_ref, o_ref):
    y1 = jnp.dot(a1_ref[...], b1_ref[...], preferred_element_type=F32)
    y2 = jnp.dot(a2_ref[...], b2_ref[...], preferred_element_type=F32)
    g1 = jax.nn.sigmoid(g1_ref[...].astype(F32))
    g2 = jax.nn.sigmoid(g2_ref[...].astype(F32))
    o_ref[...] = (g1 * y1 + g2 * y2).astype(o_ref.dtype)


def _merge_matmul(a1, b1, a2, b2, layer, u, g1_off, g2_off, tm=512, tn=512):
    M, K = a1.shape
    N = b1.shape[-1]
    tm, tn = _tile(M, tm), _tile(N, tn)
    o1, o2 = g1_off // tn, g2_off // tn
    assert g1_off % tn == 0 and g2_off % tn == 0
    return pl.pallas_call(
        _merge_body,
        grid=(M // tm, N // tn),
        in_specs=[pl.BlockSpec((tm, K), lambda i, j: (i, 0)),
                  pl.BlockSpec((None, K, tn), lambda i, j: (layer, 0, j)),
                  pl.BlockSpec((tm, K), lambda i, j: (i, 0)),
                  pl.BlockSpec((None, K, tn), lambda i, j: (layer, 0, j)),
                  pl.BlockSpec((tm, tn), lambda i, j: (i, o1 + j)),
                  pl.BlockSpec((tm, tn), lambda i, j: (i, o2 + j))],
        out_specs=pl.BlockSpec((tm, tn), lambda i, j: (i, j)),
        out_shape=jax.ShapeDtypeStruct((M, N), BF16),
        compiler_params=_cparams("parallel", "parallel"),
        name="merge_matmul",
    )(a1, b1, a2, b2, u, u)


def _split_rows(scr_ref, y, store):
    half = y.shape[0] // 2
    for g in range(y.shape[1] // LANES):
        cols = slice(g * LANES, (g + 1) * LANES)
        scr_ref[g] = y[:, cols]
        store(cols, scr_ref[g, pl.ds(0, half, stride=2), :], scr_ref[g, pl.ds(1, half, stride=2), :])


def _dwconv_body(x_ref, w_ref, s_ref, o_ref, *scratch, silu, split):
    x = x_ref[0].astype(F32)
    S = x.shape[0]
    row = lax.broadcasted_iota(jnp.int32, x.shape, 0)
    xm = jnp.where(row == 0, 0.0, pltpu.roll(x, 1, axis=0))
    xp = jnp.where(row == S - 1, 0.0, pltpu.roll(x, S - 1, axis=0))
    w = w_ref[...]
    y = w[0:1, :] * xm + w[1:2, :] * x + w[2:3, :] * xp
    if silu:
        y = y * jax.nn.sigmoid(y)
    y = y * s_ref[...]
    if split:
        def store(cols, even, odd):
            o_ref[0, 0, :, cols] = even.astype(o_ref.dtype)
            o_ref[0, 1, :, cols] = odd.astype(o_ref.dtype)
        _split_rows(scratch[0], y, store)
    else:
        o_ref[0] = y.astype(o_ref.dtype)


def _dwconv(u3, col_off, w, scale, silu, split=False, tc=256):
    B, S, _ = u3.shape
    C = w.shape[1]
    tc = _tile(C, tc)
    assert col_off % tc == 0
    o = col_off // tc
    if split:
        out_spec = pl.BlockSpec((1, 2, S // 2, tc), lambda b, j: (b, 0, 0, j))
        out_shape = jax.ShapeDtypeStruct((B, 2, S // 2, C), BF16)
    else:
        out_spec = pl.BlockSpec((1, S, tc), lambda b, j: (b, 0, j))
        out_shape = jax.ShapeDtypeStruct((B, S, C), BF16)
    return pl.pallas_call(
        functools.partial(_dwconv_body, silu=silu, split=split),
        grid=(B, C // tc),
        in_specs=[pl.BlockSpec((1, S, tc), lambda b, j: (b, 0, o + j)),
                  pl.BlockSpec((3, tc), lambda b, j: (0, j)),
                  pl.BlockSpec((1, tc), lambda b, j: (0, j))],
        out_specs=out_spec, out_shape=out_shape,
        scratch_shapes=[pltpu.VMEM((tc // LANES, S, LANES), F32)] if split else [],
        compiler_params=_cparams("parallel", "parallel"),
        name="dwconv_silu" if silu else "dwconv",
    )(u3, w.astype(F32), scale.reshape(1, C).astype(F32))


def _log_sigmoid(x):
    return jnp.minimum(x, 0.0) - jnp.log1p(jnp.exp(-jnp.abs(x)))


def _mlstm_body(q_ref, k_ref, v_ref, o_ref, grow_ref, gcol_ref, ng_ref, out_ref,
                c_ref, n_ref, m_ref, h_ref, *, L):
    S = q_ref.shape[1]
    nc = S // L
    row = lax.broadcasted_iota(jnp.int32, (L, L), 0)
    col = lax.broadcasted_iota(jnp.int32, (L, L), 1)

    c_ref[...] = jnp.zeros_like(c_ref)
    n_ref[...] = jnp.zeros_like(n_ref)
    m_ref[...] = jnp.zeros_like(m_ref)

    def direction(rev):
        d = 1 if rev else 0
        gi, gf = (2, 3) if rev else (0, 1)
        keep = (col >= row) if rev else (col <= row)
        cum_rs = ((row >= col) if rev else (row <= col)).astype(F32)
        last = 0 if rev else L - 1

        def chunk(cc, partner_done):
            r0 = pl.multiple_of(cc * L, L)
            q = q_ref[0, pl.ds(r0, L), :]
            k = k_ref[0, pl.ds(r0, L), :]
            v = v_ref[0, pl.ds(r0, L), :]
            grow = grow_ref[0, 0, :, pl.ds(r0, L)]
            gcol = gcol_ref[0, 0, pl.ds(r0, L), :]
            lf_rows = _log_sigmoid(grow)
            b_rows = jnp.dot(lf_rows, cum_rs, precision=HI, preferred_element_type=F32)
            b_r = b_rows[gf:gf + 1, :]
            b_c = jnp.sum(jnp.where(keep, lf_rows[gf:gf + 1, :], 0.0), axis=1, keepdims=True)
            li_r = grow[gi:gi + 1, :]
            li_c = gcol[:, gi:gi + 1]
            m_prev = m_ref[d]

            dlog = jnp.where(keep, b_c - b_r + li_r, -jnp.inf)
            inter = b_c + m_prev
            m_t = jnp.maximum(inter, jnp.max(dlog, axis=1, keepdims=True))
            w_intra = jnp.exp(dlog - m_t)
            w_inter = jnp.exp(inter - m_t)
            s = lax.dot_general(q, k, (((1,), (1,)), ((), ())), preferred_element_type=F32) * w_intra
            cmat = c_ref[d]
            num = w_inter * jnp.dot(q, cmat.astype(BF16), preferred_element_type=F32) \
                + jnp.dot(s.astype(BF16), v, preferred_element_type=F32)
            qn = jnp.sum(q.astype(F32) * n_ref[d], axis=1, keepdims=True)
            den = w_inter * qn + jnp.sum(s, axis=1, keepdims=True)
            h = num / jnp.maximum(jnp.abs(den), jnp.exp(-m_t))

            b_last = b_r[:, last:last + 1]
            gl = b_last - b_c + li_c
            m_new = jnp.maximum(b_last + m_prev, jnp.max(gl, axis=0, keepdims=True))
            dec = jnp.exp(b_last + m_prev - m_new)
            kw = k.astype(F32) * jnp.exp(gl - m_new)
            c_ref[d] = dec * cmat + lax.dot_general(
                kw.astype(BF16), v, (((0,), (0,)), ((), ())), preferred_element_type=F32)
            n_ref[d] = dec * n_ref[d] + jnp.sum(kw, axis=0, keepdims=True)
            m_ref[d] = m_new

            if not partner_done:
                h_ref[pl.ds(r0, L), :] = h
            else:
                hs = h_ref[pl.ds(r0, L), :] + h
                hn = hs * lax.rsqrt(jnp.mean(hs * hs, axis=1, keepdims=True) + EPS)
                gate = jax.nn.sigmoid(o_ref[0, pl.ds(r0, L), :].astype(F32))
                out_ref[0, pl.ds(r0, L), :] = (hn * ng_ref[...] * gate).astype(out_ref.dtype)

        return chunk

    fwd, bwd = direction(False), direction(True)

    def step(c, partner_done):
        fwd(c, partner_done)
        bwd(nc - 1 - c, partner_done)

    def first_half(c, carry):
        step(c, False)
        return carry

    def second_half(c, carry):
        step(c, True)
        return carry

    lax.fori_loop(0, nc // 2, first_half, 0)
    lax.fori_loop(nc // 2, nc, second_half, 0)


def _mlstm(qk, u3, v_off, o_off, grow, gcol, norm_g):
    B, S, _ = qk.shape
    H, dk, dv = M_HEADS, M_DQK, M_DV
    L = _tile(S, M_CHUNK)
    vo, oo = v_off // dv, o_off // dv
    assert v_off % dv == 0 and o_off % dv == 0
    assert (S // L) % 2 == 0, "the two directions are paired chunk by chunk"
    return pl.pallas_call(
        functools.partial(_mlstm_body, L=L),
        grid=(B, H),
        in_specs=[pl.BlockSpec((1, S, dk), lambda b, h: (b, 0, h)),
                  pl.BlockSpec((1, S, dk), lambda b, h: (b, 0, H + h)),
                  pl.BlockSpec((1, S, dv), lambda b, h: (b, 0, vo + h)),
                  pl.BlockSpec((1, S, dv), lambda b, h: (b, 0, oo + h)),
                  pl.BlockSpec((1, 1, 8, S), lambda b, h: (b, h, 0, 0)),
                  pl.BlockSpec((1, 1, S, 8), lambda b, h: (b, h, 0, 0)),
                  pl.BlockSpec((1, dv), lambda b, h: (0, h))],
        out_specs=pl.BlockSpec((1, S, dv), lambda b, h: (b, 0, h)),
        out_shape=jax.ShapeDtypeStruct((B, S, H * dv), BF16),
        scratch_shapes=[pltpu.VMEM((2, dk, dv), F32), pltpu.VMEM((2, 1, dk), F32),
                        pltpu.VMEM((2, 1, 1), F32), pltpu.VMEM((S, dv), F32)],
        compiler_params=_cparams("parallel", "parallel"),
        name="mlstm",
    )(qk, qk, u3, u3, grow, gcol, norm_g.reshape(1, H * dv).astype(F32))


def _hy_mlp_body(z_ref, w1_ref, b1_ref, w2_ref, b2_ref, fr_ref, o_ref):
    fr = fr_ref[...]
    h = jnp.sin(fr * (jnp.dot(z_ref[...], w1_ref[...], precision=HI, preferred_element_type=F32) + b1_ref[...]))
    h = jnp.sin(fr * (jnp.dot(h, w2_ref[...], precision=HI, preferred_element_type=F32) + b2_ref[...]))
    o_ref[...] = h


def _hy_filter_body(h_ref, wf_ref, wb_ref, bf_ref, bb_ref, df_ref, db_ref, hs_ref, hd_ref, scr_ref):
    L = h_ref.shape[0]
    h2 = h_ref[...]
    tc = wf_ref.shape[1]
    row = lax.broadcasted_iota(jnp.int32, (L, tc), 0)
    tn = row.astype(F32) / L
    hf = (jnp.dot(h2, wf_ref[...], precision=HI, preferred_element_type=F32) + bf_ref[...]) \
        * jnp.exp(-tn * jnp.abs(df_ref[...]))
    hb = (jnp.dot(h2, wb_ref[...], precision=HI, preferred_element_type=F32) + bb_ref[...]) \
        * jnp.exp(-tn * jnp.abs(db_ref[...]))
    hb = jnp.where(row == 0, 0.0, hb)
    norm = jnp.sum(jnp.abs(hf), axis=0, keepdims=True) + jnp.sum(jnp.abs(hb), axis=0, keepdims=True)
    inv = 1.0 / norm
    for o_ref, val in ((hs_ref, (hf + hb) * inv), (hd_ref, (hb - hf) * inv)):
        def store(cols, even, odd, o_ref=o_ref):
            o_ref[0, :, cols] = even.astype(o_ref.dtype)
            o_ref[1, :, cols] = odd.astype(o_ref.dtype)
        _split_rows(scr_ref, val, store)


def _hy_filters(L, w1, b1, w2, b2, w3, b3, freq, decay, tc=256):
    hid = w1.shape[1]
    C = w3.shape[1] // (2 * H_ORDER)
    tn = jnp.arange(L, dtype=F32) / L
    bands = jnp.linspace(1e-4, H_BANDS - 1, H_BANDS, dtype=F32)
    ang = (2.0 * math.pi) * tn[:, None] * bands[None, :]
    z = jnp.concatenate([tn[:, None], jnp.cos(ang), jnp.sin(ang)], axis=-1)
    emb = z.shape[1]
    z = jnp.pad(z, ((0, 0), (0, LANES - emb)))
    w1p = jnp.pad(w1.astype(F32), ((0, LANES - emb), (0, 0)))
    row = lambda a: a.reshape(1, -1).astype(F32)
    h2 = pl.pallas_call(
        _hy_mlp_body,
        out_shape=jax.ShapeDtypeStruct((L, hid), F32),
        compiler_params=pltpu.CompilerParams(vmem_limit_bytes=VMEM_LIMIT),
        name="hyena_mlp",
    )(z, w1p, row(b1), w2.astype(F32), row(b2), row(freq))
    tc = _tile(C, tc)
    nct = C // tc
    fwd = lambda o, j: (0, o * nct + j)
    bwd = lambda o, j: (0, (H_ORDER + o) * nct + j)
    out = pl.BlockSpec((2, L // 2, tc), lambda o, j: (0, 0, o * nct + j))
    w3 = w3.astype(F32)
    b3r, dcr = row(b3), row(decay)
    return pl.pallas_call(
        _hy_filter_body,
        grid=(H_ORDER, nct),
        in_specs=[pl.BlockSpec((L, hid), lambda o, j: (0, 0)),
                  pl.BlockSpec((hid, tc), fwd), pl.BlockSpec((hid, tc), bwd),
                  pl.BlockSpec((1, tc), fwd), pl.BlockSpec((1, tc), bwd),
                  pl.BlockSpec((1, tc), fwd), pl.BlockSpec((1, tc), bwd)],
        out_specs=[out, out],
        out_shape=[jax.ShapeDtypeStruct((2, L // 2, H_ORDER * C), BF16)] * 2,
        scratch_shapes=[pltpu.VMEM((tc // LANES, L, LANES), F32)],
        compiler_params=_cparams("parallel", "parallel"),
        name="hyena_filter",
    )(h2, w3, w3, b3r, b3r, dcr, dcr)


def _dft_tables(L):
    L2 = L // 2
    kap = jnp.arange(L2, dtype=jnp.int32)
    m = ((2 * kap[:, None] + 1) * kap[None, :]) % (2 * L)
    ang = m.astype(F32) * F32(math.pi / L)
    gc, gs = jnp.cos(ang), jnp.sin(ang)
    w_lo = (2 * kap + 1).astype(F32) * F32(math.pi / (2 * L))
    w_hi = (2 * (L - 1 - kap) + 1).astype(F32) * F32(math.pi / (2 * L))
    tw = jnp.stack([jnp.cos(w_lo), jnp.sin(w_lo), jnp.cos(w_hi), jnp.sin(w_hi)], axis=1)
    return gc.astype(BF16), gs.astype(BF16), gc.T.astype(BF16), gs.T.astype(BF16), tw


def _twiddles(tw_ref):
    tw = tw_ref[...]
    return tw[:, 0:1], tw[:, 1:2], tw[:, 2:3], tw[:, 3:4]


def _hy_kspec_body(gc_ref, gs_ref, ae_ref, ao_ref, be_ref, bo_ref, tw_ref, kre_ref, kim_ref):
    gc, gs = gc_ref[...], gs_ref[...]
    c, s, ch, sh = _twiddles(tw_ref)
    dot = functools.partial(jnp.dot, preferred_element_type=F32)
    p1, p2, p3 = dot(gc, ae_ref[...]), dot(gc, ao_ref[...]), dot(gs, ao_ref[...])
    p4, p5, p6 = dot(gs, be_ref[...]), dot(gc, bo_ref[...]), dot(gs, bo_ref[...])
    kre_ref[0] = p1 + c * p2 - s * p3
    kim_ref[0] = p4 + s * p5 + c * p6
    kre_ref[1] = p1 + ch * p2 + sh * p3
    kim_ref[1] = -p4 + sh * p5 - ch * p6


def _hy_kspec(gc, gs, tw, hsum, hdiff, tm=512, tn=256):
    _, L2, OC = hsum.shape
    a2, b2 = hsum, hdiff
    tm, tn = _tile(L2, tm), _tile(OC, tn)
    nct = OC // tn
    tab = pl.BlockSpec((tm, L2), lambda j, i: (i, 0))
    even = pl.BlockSpec((None, L2, tn), lambda j, i: (0, 0, j))
    odd = pl.BlockSpec((None, L2, tn), lambda j, i: (1, 0, j))
    out = pl.BlockSpec((2, tm, tn), lambda j, i: (0, i, j))
    shp = jax.ShapeDtypeStruct((2, L2, OC), F32)
    return pl.pallas_call(
        _hy_kspec_body,
        grid=(nct, L2 // tm),
        in_specs=[tab, tab, even, odd, even, odd, pl.BlockSpec((tm, 4), lambda j, i: (i, 0))],
        out_specs=[out, out], out_shape=[shp, shp],
        compiler_params=_cparams("parallel", "parallel"),
        name="hyena_kspec",
    )(gc, gs, a2, a2, b2, b2, tw)


def _hy_fwd_body(gc_ref, gs_ref, ze_ref, zo_ref, kre_ref, kim_ref, tw_ref, wre_ref, wim_ref):
    gc, gs = gc_ref[...], gs_ref[...]
    c, s, ch, sh = _twiddles(tw_ref)
    kre_l, kim_l, kre_h, kim_h = kre_ref[0], kim_ref[0], kre_ref[1], kim_ref[1]
    dot = functools.partial(jnp.dot, preferred_element_type=F32)
    for b in range(ze_ref.shape[0]):
        r0, s0 = dot(gc, ze_ref[b]), dot(gs, ze_ref[b])
        r1, s1 = dot(gc, zo_ref[b]), dot(gs, zo_ref[b])
        zre_l = r0 + c * r1 - s * s1
        zim_l = -s0 - c * s1 - s * r1
        zre_h = r0 + ch * r1 + sh * s1
        zim_h = s0 + ch * s1 - sh * r1
        a = zre_l * kre_l - zim_l * kim_l
        bb = zre_l * kim_l + zim_l * kre_l
        p = zre_h * kre_h - zim_h * kim_h
        q = zre_h * kim_h + zim_h * kre_h
        wre_ref[b, 0] = (a + p).astype(wre_ref.dtype)
        wim_ref[b, 0] = (bb - q).astype(wim_ref.dtype)
        wre_ref[b, 1] = (a * c - bb * s + p * ch - q * sh).astype(wre_ref.dtype)
        wim_ref[b, 1] = (a * s + bb * c - p * sh - q * ch).astype(wim_ref.dtype)


def _hy_fwd(gc, gs, tw, z4, z_off, kre, kim, order, C, tm=512, tn=256):
    B, _, L2, _ = z4.shape
    tm, tn = _tile(L2, tm), _tile(C, tn)
    nct = C // tn
    zo = z_off // tn
    assert z_off % tn == 0
    tab = pl.BlockSpec((tm, L2), lambda j, i: (i, 0))
    ksp = pl.BlockSpec((2, tm, tn), lambda j, i: (0, i, order * nct + j))
    out = pl.BlockSpec((B, 2, tm, tn), lambda j, i: (0, 0, i, j))
    shp = jax.ShapeDtypeStruct((B, 2, L2, C), BF16)
    return pl.pallas_call(
        _hy_fwd_body,
        grid=(nct, L2 // tm),
        in_specs=[tab, tab,
                  pl.BlockSpec((B, None, L2, tn), lambda j, i: (0, 0, 0, zo + j)),
                  pl.BlockSpec((B, None, L2, tn), lambda j, i: (0, 1, 0, zo + j)),
                  ksp, ksp, pl.BlockSpec((tm, 4), lambda j, i: (i, 0))],
        out_specs=[out, out], out_shape=[shp, shp],
        compiler_params=_cparams("parallel", "parallel"),
        name="hyena_fwd",
    )(gc, gs, z4, z4, kre, kim, tw)


def _hy_inv_body(gct_ref, gst_ref, wre_ref, wim_ref, z_ref, g_ref, skip_ref, o_ref, *scratch, scale, interleave):
    gct, gst = gct_ref[...], gst_ref[...]
    skip = skip_ref[...]
    tm = gct.shape[0]
    dot = functools.partial(jnp.dot, preferred_element_type=F32)
    for b in range(z_ref.shape[0]):
        for j in range(2):
            y = dot(gct, wre_ref[b, j]) - dot(gst, wim_ref[b, j])
            out = g_ref[b, j].astype(F32) * (scale * y + skip * z_ref[b, j].astype(F32))
            if interleave:
                for g in range(out.shape[1] // LANES):
                    scratch[0][g, pl.ds(j, tm, stride=2), :] = out[:, g * LANES:(g + 1) * LANES]
            else:
                o_ref[b, j] = out.astype(o_ref.dtype)
        if interleave:
            for g in range(o_ref.shape[2] // LANES):
                o_ref[b, :, g * LANES:(g + 1) * LANES] = scratch[0][g].astype(o_ref.dtype)


def _hy_inv(gct, gst, wre, wim, z4, z_off, g4, g_off, skip, interleave, tm=512, tn=256):
    B, _, L2, C = wre.shape
    tm, tn = _tile(L2, tm), _tile(C, tn)
    zo, go = z_off // tn, g_off // tn
    assert z_off % tn == 0 and g_off % tn == 0
    tab = pl.BlockSpec((tm, L2), lambda j, i: (i, 0))
    wsp = pl.BlockSpec((B, 2, L2, tn), lambda j, i: (0, 0, 0, j))
    rows = lambda off: pl.BlockSpec((B, 2, tm, tn), lambda j, i: (0, 0, i, off + j))
    if interleave:
        out_spec = pl.BlockSpec((B, 2 * tm, tn), lambda j, i: (0, i, j))
        out_shape = jax.ShapeDtypeStruct((B, 2 * L2, C), BF16)
    else:
        out_spec, out_shape = rows(0), jax.ShapeDtypeStruct((B, 2, L2, C), BF16)
    return pl.pallas_call(
        functools.partial(_hy_inv_body, scale=1.0 / (2 * L2), interleave=interleave),
        grid=(C // tn, L2 // tm),
        in_specs=[tab, tab, wsp, wsp, rows(zo), rows(go), pl.BlockSpec((1, tn), lambda j, i: (0, j))],
        out_specs=out_spec, out_shape=out_shape,
        scratch_shapes=[pltpu.VMEM((tn // LANES, 2 * tm, LANES), F32)] if interleave else [],
        compiler_params=_cparams("parallel", "parallel"),
        name="hyena_inv",
    )(gct, gst, wre, wim, z4, g4, skip.reshape(1, C).astype(F32))


def _top_values(x, n):
    rank = lax.broadcasted_iota(jnp.int32, (n, x.shape[1]), 0)
    vals = jnp.zeros((n, x.shape[1]), F32)
    work = x
    for r in range(n):
        mx = jnp.max(work, axis=0, keepdims=True)
        vals = jnp.where(rank == r, mx, vals)
        work = jnp.where(work >= mx, -jnp.inf, work)
    return vals, mx


def _peer_route_body(q_ref, k_ref, s1_ref, e1_ref, th_ref, e0_ref):
    nt = (((1,), (1,)), ((), ()))
    s0 = lax.dot_general(k_ref[0], q_ref[0], nt, preferred_element_type=F32)
    s1 = lax.dot_general(k_ref[1], q_ref[1], nt, preferred_element_type=F32)
    v0, _ = _top_values(s0, P_TOPK)
    v1, _ = _top_values(s1, P_TOPK)
    k, oct_ = P_TOPK, 8
    assert k == 2 * oct_
    pieces = [v0[0:1, :] + v1]
    pieces += [v0[a:a + 1, :] + v1[0:oct_, :] for a in range(1, oct_)]
    pieces += [v0[oct_:k, :] + v1[0:1, :]]
    cand = jnp.concatenate(pieces, axis=0)
    _, tau = _top_values(cand, P_TOPK)
    top = v0[0:1, :] + v1[0:1, :]
    zsum = jnp.sum(jnp.where(cand >= tau, jnp.exp(cand - top), 0.0), axis=0, keepdims=True)
    th = jnp.full(s0.shape, jnp.inf, F32)
    for b in range(P_TOPK):
        vb = v1[b:b + 1, :]
        th = jnp.where(s0 + vb >= tau, vb, th)
    s1_ref[0] = s1
    e1_ref[0] = jnp.exp(s1 - v1[0:1, :])
    th_ref[0] = th
    e0_ref[0] = jnp.exp(s0 - v0[0:1, :]) / zsum


def _peer_route(q3, keys, tm=256):
    H2, T, dk = q3.shape
    H = H2 // 2
    tm = _tile(T, tm)
    big = pl.BlockSpec((1, N_KEYS, tm), lambda i, h: (h, 0, i))
    shp = jax.ShapeDtypeStruct((H, N_KEYS, T), F32)
    return pl.pallas_call(
        _peer_route_body,
        grid=(T // tm, H),
        in_specs=[pl.BlockSpec((2, tm, dk), lambda i, h: (h, i, 0)),
                  pl.BlockSpec((2, N_KEYS, dk), lambda i, h: (h, 0, 0))],
        out_specs=[big] * 4, out_shape=[shp] * 4,
        compiler_params=_cparams("parallel", "parallel"),
        name="peer_route",
    )(q3, keys)


GATE_ROWS = 32


def _peer_expert_body(xt_ref, u_ref, vt_ref, s1_ref, e1_ref, th_ref, e0_ref, o_ref, a_ref, w_ref, *, te):
    e = pl.program_id(1)
    tm = a_ref.shape[1]

    @pl.when(e == 0)
    def _():
        o_ref[...] = jnp.zeros_like(o_ref)

    half = tm // 2 if tm % (2 * LANES) == 0 else tm
    halves = [slice(s, s + half) for s in range(0, tm, half)]

    rows_i = []
    for ii in range(te // N_KEYS):
        i_idx = e * (te // N_KEYS) + ii
        rows_i.append(([th_ref[h, pl.ds(i_idx, 1), :] for h in range(P_HEADS)],
                       [e0_ref[h, pl.ds(i_idx, 1), :] for h in range(P_HEADS)]))

    def scores(hs):
        a_ref[:, hs] = jnp.dot(u_ref[...], xt_ref[:, hs], preferred_element_type=F32)

    def weights(hs):
        for ii in range(te // N_KEYS):
            th_full, e0_full = rows_i[ii]
            for tc in range(hs.start // LANES, hs.stop // LANES):
                cols = slice(tc * LANES, (tc + 1) * LANES)
                th_rows = [r[:, cols] for r in th_full]
                e0_rows = [r[:, cols] for r in e0_full]
                for jb in range(N_KEYS // GATE_ROWS):
                    rows = slice(jb * GATE_ROWS, (jb + 1) * GATE_ROWS)
                    g = None
                    for h in range(P_HEADS):
                        term = e0_rows[h] * jnp.where(s1_ref[h, rows, cols] >= th_rows[h],
                                                      e1_ref[h, rows, cols], 0.0)
                        g = term if g is None else g + term
                    erows = slice(ii * N_KEYS + jb * GATE_ROWS, ii * N_KEYS + (jb + 1) * GATE_ROWS)
                    a = a_ref[erows, cols]
                    act = 0.5 * a * (1.0 + lax.erf(a * (1.0 / math.sqrt(2.0))))
                    w_ref[erows, cols] = (g * act).astype(BF16)

    def project(hs):
        o_ref[:, hs] += jnp.dot(vt_ref[...], w_ref[:, hs], preferred_element_type=F32)

    for hs in halves:
        scores(hs)
        weights(hs)
    for hs in halves:
        project(hs)


def _peer_expert(xt, U, Vt, layer, s1, e1, th, e0, tm=512, te=512):
    D, T = xt.shape
    E = U.shape[1]
    tm, te = _tile(T, tm), _tile(E, te)
    H = s1.shape[0]
    once = dict(pipeline_mode=pl.Buffered(1))
    rt = pl.BlockSpec((H, N_KEYS, tm), lambda i, e: (0, 0, i), **once)
    return pl.pallas_call(
        functools.partial(_peer_expert_body, te=te),
        grid=(T // tm, E // te),
        in_specs=[pl.BlockSpec((D, tm), lambda i, e: (0, i), **once),
                  pl.BlockSpec((None, te, D), lambda i, e: (layer, e, 0)),
                  pl.BlockSpec((None, D, te), lambda i, e: (layer, 0, e)),
                  rt, rt, rt, rt],
        out_specs=pl.BlockSpec((D, tm), lambda i, e: (0, i)),
        out_shape=jax.ShapeDtypeStruct((D, T), F32),
        scratch_shapes=[pltpu.VMEM((te, tm), F32), pltpu.VMEM((te, tm), BF16)],
        compiler_params=_cparams("parallel", "arbitrary"),
        name="peer_expert",
    )(xt, U, Vt, s1, e1, th, e0)


def _layer(l, x2, xn, B, S, P, tables):
    T, D = x2.shape
    H, dk, dv = M_HEADS, M_DQK, M_DV
    mqk, mv = H * dk, H * dv
    n_gate = 4 * H
    C = P["hy_skip"].shape[2]
    m_cols = 2 * mqk + 2 * mv + n_gate
    assert (m_cols - n_gate) % LANES == 0
    w_in = P["w_in"][l]

    u_m = _matmul(xn, w_in[:, :m_cols - n_gate].astype(BF16), BF16, name="in_proj_mlstm")
    w_gate = jnp.pad(w_in[:, m_cols - n_gate:m_cols], ((0, 0), (0, LANES - n_gate))).astype(BF16)
    b_gate = jnp.pad(P["m_gate_b"][l], (0, LANES - n_gate))
    gates = _matmul(xn, w_gate, F32, bias=b_gate, name="gate_proj")
    u_hg = _matmul(xn, w_in[:, m_cols:].astype(BF16), BF16, name="in_proj_hyena")
    um3 = u_m.reshape(B, S, u_m.shape[1])

    qk_scale = jnp.concatenate([jnp.ones((mqk,), F32), jnp.full((mqk,), dk ** -0.5, F32)])
    qk = _dwconv(um3, 0, P["m_conv"][l], qk_scale, silu=True)
    g5 = gates[:, :n_gate].reshape(B, S, 4, H)
    grow = jnp.pad(jnp.transpose(g5, (0, 3, 2, 1)), ((0, 0), (0, 0), (0, 4), (0, 0)))
    gcol = jnp.pad(jnp.transpose(g5, (0, 3, 1, 2)), ((0, 0), (0, 0), (0, 0), (0, 4)))
    hm = _mlstm(qk, um3, 2 * mqk, 2 * mqk + mv, grow, gcol, P["m_norm_g"][l])

    gc, gs, gct, gst, tw = tables
    nh = (H_ORDER + 1) * C
    hy4 = _dwconv(u_hg.reshape(B, S, u_hg.shape[1]), 0, P["hy_conv"][l], jnp.ones((nh,), F32), silu=False,
                  split=True)
    hsum, hdiff = _hy_filters(S, P["hy_w1"][l], P["hy_b1"][l], P["hy_w2"][l], P["hy_b2"][l], P["hy_w3"][l],
                              P["hy_b3"][l], P["hy_freq"][l], P["hy_decay"][l])
    kre, kim = _hy_kspec(gc, gs, tw, hsum, hdiff)
    z4, z_off = hy4, 0
    for o in range(H_ORDER):
        wre, wim = _hy_fwd(gc, gs, tw, z4, z_off, kre, kim, o, C)
        z4 = _hy_inv(gct, gst, wre, wim, z4, z_off, hy4, (o + 1) * C, P["hy_skip"][l, o],
                     interleave=o == H_ORDER - 1)
        z_off = 0
    hy = z4.reshape(T, C)

    merged = _merge_matmul(hm.reshape(T, mv), P["w_mo_bf16"], hy, P["w_ho_bf16"], l, u_hg, nh, nh + D)
    x2 = _matmul(merged, P["w_o"], F32, layer=l, residual=x2, tn=512, name="out_proj")

    xn2, xn2t = _rmsnorm(x2, P["ln2_g"][l], BF16, want_t=True)
    q = _matmul(xn2, P["peer_wq"], BF16, layer=l, tn=512, name="peer_query")
    q3 = jnp.transpose(q.reshape(T, 2 * P_HEADS, P_DKEY), (1, 0, 2))
    keys = P["peer_keys"][l].reshape(2 * P_HEADS, N_KEYS, P_DKEY).astype(BF16)
    s1, e1, th, e0 = _peer_route(q3, keys)
    yt = _peer_expert(xn2t, P["peer_u_bf16"], P["peer_vt_bf16"], l, s1, e1, th, e0)
    return x2, yt


_LAYER_PARAMS = ("ln1_g", "ln2_g", "w_in", "m_conv", "m_gate_b", "m_norm_g", "hy_conv", "hy_w1", "hy_b1",
                 "hy_w2", "hy_b2", "hy_w3", "hy_b3", "hy_freq", "hy_decay", "hy_skip", "w_mo", "w_ho", "w_o",
                 "peer_wq", "peer_keys", "peer_u", "peer_v")


def kernel(x, ln1_g, ln2_g, w_in, m_conv, m_gate_b, m_norm_g, hy_conv, hy_w1, hy_b1, hy_w2, hy_b2, hy_w3, hy_b3,
           hy_freq, hy_decay, hy_skip, w_mo, w_ho, w_o, peer_wq, peer_keys, peer_u, peer_v, final_g):
    P = dict(zip(_LAYER_PARAMS, (ln1_g, ln2_g, w_in, m_conv, m_gate_b, m_norm_g, hy_conv, hy_w1, hy_b1,
                                 hy_w2, hy_b2, hy_w3, hy_b3, hy_freq, hy_decay, hy_skip, w_mo, w_ho, w_o,
                                 peer_wq, peer_keys, peer_u, peer_v)))
    P["w_mo_bf16"], P["w_ho_bf16"] = w_mo.astype(BF16), w_ho.astype(BF16)
    P["peer_u_bf16"] = peer_u.astype(BF16)
    P["peer_vt_bf16"] = jnp.swapaxes(peer_v, 1, 2).astype(BF16)
    B, S, D = x.shape
    depth = ln1_g.shape[0]
    tables = _dft_tables(S)
    x2 = x.reshape(B * S, D)
    (xn,) = _rmsnorm(x2, ln1_g[0], BF16)
    for l in range(depth):
        x2, yt = _layer(l, x2, xn, B, S, P, tables)
        if l + 1 < depth:
            x2, xn = _rmsnorm(x2, ln1_g[l + 1], BF16, add_t=yt, want_sum=True)
        else:
            (out,) = _rmsnorm(x2, final_g, F32, add_t=yt)
    return out.reshape(B, S, D)
```

```python
import functools
import math

import jax
import jax.numpy as jnp
from jax import lax
from jax.experimental import pallas as pl
from jax.experimental.pallas import tpu as pltpu

F32 = jnp.float32
BF16 = jnp.bfloat16

EPS = 1e-6
M_HEADS = 8
M_DQK = 256
M_DV = 512
M_CHUNK = 256
H_ORDER = 2
H_BANDS = 16
P_HEADS = 8
N_KEYS = 128
P_DKEY = 128
P_TOPK = 16

V7X_VMEM_BYTES = 64 * 1024 * 1024
VMEM_LIMIT = V7X_VMEM_BYTES - 8 * 1024 * 1024
LANES = 128
HI = lax.Precision.HIGHEST


def _cparams(*sem):
    return pltpu.CompilerParams(dimension_semantics=sem, vmem_limit_bytes=VMEM_LIMIT)


def _tile(dim, pref):
    t = min(dim, pref)
    while dim % t:
        t -= LANES
    assert t > 0 and dim % t == 0, (dim, pref)
    return t


def _rmsnorm_body(*refs, has_add, want_sum, want_t):
    x_ref, g_ref = refs[0], refs[1]
    pos = 2
    x = x_ref[...]
    if has_add:
        x = x + refs[pos][...].T
        pos += 1
    if want_sum:
        refs[pos][...] = x
        pos += 1
    ms = jnp.mean(x * x, axis=-1, keepdims=True)
    y = x * lax.rsqrt(ms + EPS) * g_ref[...]
    refs[pos][...] = y.astype(refs[pos].dtype)
    if want_t:
        refs[pos + 1][...] = y.T.astype(refs[pos + 1].dtype)


def _rmsnorm(x, g, out_dtype, add_t=None, want_sum=False, want_t=False):
    T, D = x.shape
    tm = _tile(T, 128)
    row = pl.BlockSpec((tm, D), lambda i: (i, 0))
    colT = pl.BlockSpec((D, tm), lambda i: (0, i))
    in_specs = [row, pl.BlockSpec((1, D), lambda i: (0, 0))]
    args = [x, g.reshape(1, D).astype(F32)]
    if add_t is not None:
        in_specs.append(colT)
        args.append(add_t)
    out_specs, out_shape = [], []
    if want_sum:
        out_specs.append(row)
        out_shape.append(jax.ShapeDtypeStruct((T, D), F32))
    out_specs.append(row)
    out_shape.append(jax.ShapeDtypeStruct((T, D), out_dtype))
    if want_t:
        out_specs.append(colT)
        out_shape.append(jax.ShapeDtypeStruct((D, T), out_dtype))
    return pl.pallas_call(
        functools.partial(_rmsnorm_body, has_add=add_t is not None, want_sum=want_sum, want_t=want_t),
        grid=(T // tm,),
        in_specs=in_specs, out_specs=out_specs, out_shape=out_shape,
        compiler_params=_cparams("parallel"),
        name="rmsnorm",
    )(*args)


def _mm_body(a_ref, b_ref, *rest, has_bias, has_res, cast_b):
    rest = list(rest)
    bq_ref = rest.pop() if cast_b else None
    o_ref = rest.pop()
    if cast_b:
        @pl.when(pl.program_id(1) == 0)
        def _():
            bq_ref[...] = b_ref[...].astype(BF16)
        b = bq_ref[...]
    else:
        b = b_ref[...]
    acc = jnp.dot(a_ref[...], b, preferred_element_type=F32)
    if has_bias:
        acc = acc + rest[0][...]
    if has_res:
        acc = acc + rest[1 if has_bias else 0][...]
    o_ref[...] = acc.astype(o_ref.dtype)


def _matmul(a, b, out_dtype, *, layer=None, n_cols=None, col_off=0, bias=None, residual=None,
            tm=512, tn=1024, name="matmul"):
    M, K = a.shape
    Nb = b.shape[-1]
    N = n_cols if n_cols is not None else Nb
    cast_b = b.dtype != BF16
    tm, tn = _tile(M, tm), _tile(N, tn)
    assert col_off % tn == 0
    co = col_off // tn
    n_outer = cast_b or a.size * (N // tn) + K * N < a.size + K * N * (M // tm)
    if n_outer:
        grid = (N // tn, M // tm)
        im = lambda j, i: (i, 0)
        jn = lambda j, i: (0, j)
        ij = lambda j, i: (i, j)
        jb2 = lambda j, i: (0, co + j)
        jb3 = lambda j, i: (layer, 0, co + j)
    else:
        grid = (M // tm, N // tn)
        im = lambda i, j: (i, 0)
        jn = lambda i, j: (0, j)
        ij = lambda i, j: (i, j)
        jb2 = lambda i, j: (0, co + j)
        jb3 = lambda i, j: (layer, 0, co + j)
    b_spec = pl.BlockSpec((K, tn), jb2) if b.ndim == 2 else pl.BlockSpec((None, K, tn), jb3)
    in_specs = [pl.BlockSpec((tm, K), im), b_spec]
    args = [a, b]
    if bias is not None:
        in_specs.append(pl.BlockSpec((1, tn), jn))
        args.append(bias.reshape(1, N).astype(F32))
    if residual is not None:
        in_specs.append(pl.BlockSpec((tm, tn), ij))
        args.append(residual)
    return pl.pallas_call(
        functools.partial(_mm_body, has_bias=bias is not None, has_res=residual is not None, cast_b=cast_b),
        grid=grid,
        in_specs=in_specs,
        out_specs=pl.BlockSpec((tm, tn), ij),
        out_shape=jax.ShapeDtypeStruct((M, N), out_dtype),
        scratch_shapes=[pltpu.VMEM((K, tn), BF16)] if cast_b else [],
        compiler_params=_cparams("parallel", "arbitrary" if cast_b else "parallel"),
        name=name,
    )(*args)


def _merge_body(a1_ref, b1_ref, a2_ref, b2_ref, g1_ref, g2_ref, o_ref):
    y1 = jnp.dot(a1_ref[...], b1_ref[...], preferred_element_type=F32)
    y2 = jnp.dot(a2_ref[...], b2_ref[...], preferred_element_type=F32)
    g1 = jax.nn.sigmoid(g1_ref[...].astype(F32))
    g2 = jax.nn.sigmoid(g2_ref[...].astype(F32))
    o_ref[...] = (g1 * y1 + g2 * y2).astype(o_ref.dtype)


def _merge_matmul(a1, b1, a2, b2, layer, u, g1_off, g2_off, tm=512, tn=512):
    M, K = a1.shape
    N = b1.shape[-1]
    tm, tn = _tile(M, tm), _tile(N, tn)
    o1, o2 = g1_off // tn, g2_off // tn
    assert g1_off % tn == 0 and g2_off % tn == 0
    return pl.pallas_call(
        _merge_body,
        grid=(M // tm, N // tn),
        in_specs=[pl.BlockSpec((tm, K), lambda i, j: (i, 0)),
                  pl.BlockSpec((None, K, tn), lambda i, j: (layer, 0, j)),
                  pl.BlockSpec((tm, K), lambda i, j: (i, 0)),
                  pl.BlockSpec((None, K, tn), lambda i, j: (layer, 0, j)),
                  pl.BlockSpec((tm, tn), lambda i, j: (i, o1 + j)),
                  pl.BlockSpec((tm, tn), lambda i, j: (i, o2 + j))],
        out_specs=pl.BlockSpec((tm, tn), lambda i, j: (i, j)),
        out_shape=jax.ShapeDtypeStruct((M, N), BF16),
        compiler_params=_cparams("parallel", "parallel"),
        name="merge_matmul",
    )(a1, b1, a2, b2, u, u)


def _split_rows(scr_ref, y, store):
    half = y.shape[0] // 2
    for g in range(y.shape[1] // LANES):
        cols = slice(g * LANES, (g + 1) * LANES)
        scr_ref[g] = y[:, cols]
        store(cols, scr_ref[g, pl.ds(0, half, stride=2), :], scr_ref[g, pl.ds(1, half, stride=2), :])


def _dwconv_body(x_ref, w_ref, s_ref, o_ref, *scratch, silu, split):
    x = x_ref[0].astype(F32)
    S = x.shape[0]
    row = lax.broadcasted_iota(jnp.int32, x.shape, 0)
    xm = jnp.where(row == 0, 0.0, pltpu.roll(x, 1, axis=0))
    xp = jnp.where(row == S - 1, 0.0, pltpu.roll(x, S - 1, axis=0))
    w = w_ref[...]
    y = w[0:1, :] * xm + w[1:2, :] * x + w[2:3, :] * xp
    if silu:
        y = y * jax.nn.sigmoid(y)
    y = y * s_ref[...]
    if split:
        def store(cols, even, odd):
            o_ref[0, 0, :, cols] = even.astype(o_ref.dtype)
            o_ref[0, 1, :, cols] = odd.astype(o_ref.dtype)
        _split_rows(scratch[0], y, store)
    else:
        o_ref[0] = y.astype(o_ref.dtype)


def _dwconv(u3, col_off, w, scale, silu, split=False, tc=256):
    B, S, _ = u3.shape
    C = w.shape[1]
    tc = _tile(C, tc)
    assert col_off % tc == 0
    o = col_off // tc
    if split:
        out_spec = pl.BlockSpec((1, 2, S // 2, tc), lambda b, j: (b, 0, 0, j))
        out_shape = jax.ShapeDtypeStruct((B, 2, S // 2, C), BF16)
    else:
        out_spec = pl.BlockSpec((1, S, tc), lambda b, j: (b, 0, j))
        out_shape = jax.ShapeDtypeStruct((B, S, C), BF16)
    return pl.pallas_call(
        functools.partial(_dwconv_body, silu=silu, split=split),
        grid=(B, C // tc),
        in_specs=[pl.BlockSpec((1, S, tc), lambda b, j: (b, 0, o + j)),
                  pl.BlockSpec((3, tc), lambda b, j: (0, j)),
                  pl.BlockSpec((1, tc), lambda b, j: (0, j))],
        out_specs=out_spec, out_shape=out_shape,
        scratch_shapes=[pltpu.VMEM((tc // LANES, S, LANES), F32)] if split else [],
        compiler_params=_cparams("parallel", "parallel"),
        name="dwconv_silu" if silu else "dwconv",
    )(u3, w.astype(F32), scale.reshape(1, C).astype(F32))


def _log_sigmoid(x):
    return jnp.minimum(x, 0.0) - jnp.log1p(jnp.exp(-jnp.abs(x)))


def _mlstm_body(q_ref, k_ref, v_ref, o_ref, grow_ref, gcol_ref, ng_ref, out_ref,
                c_ref, n_ref, m_ref, h_ref, *, L):
    S = q_ref.shape[1]
    nc = S // L
    row = lax.broadcasted_iota(jnp.int32, (L, L), 0)
    col = lax.broadcasted_iota(jnp.int32, (L, L), 1)

    c_ref[...] = jnp.zeros_like(c_ref)
    n_ref[...] = jnp.zeros_like(n_ref)
    m_ref[...] = jnp.zeros_like(m_ref)

    def direction(rev):
        d = 1 if rev else 0
        gi, gf = (2, 3) if rev else (0, 1)
        keep = (col >= row) if rev else (col <= row)
        cum_rs = ((row >= col) if rev else (row <= col)).astype(F32)
        last = 0 if rev else L - 1

        def chunk(cc, partner_done):
            r0 = pl.multiple_of(cc * L, L)
            q = q_ref[0, pl.ds(r0, L), :]
            k = k_ref[0, pl.ds(r0, L), :]
            v = v_ref[0, pl.ds(r0, L), :]
            grow = grow_ref[0, 0, :, pl.ds(r0, L)]
            gcol = gcol_ref[0, 0, pl.ds(r0, L), :]
            lf_rows = _log_sigmoid(grow)
            b_rows = jnp.dot(lf_rows, cum_rs, precision=HI, preferred_element_type=F32)
            b_r = b_rows[gf:gf + 1, :]
            b_c = jnp.sum(jnp.where(keep, lf_rows[gf:gf + 1, :], 0.0), axis=1, keepdims=True)
            li_r = grow[gi:gi + 1, :]
            li_c = gcol[:, gi:gi + 1]
            m_prev = m_ref[d]

            dlog = jnp.where(keep, b_c - b_r + li_r, -jnp.inf)
            inter = b_c + m_prev
            m_t = jnp.maximum(inter, jnp.max(dlog, axis=1, keepdims=True))
            w_intra = jnp.exp(dlog - m_t)
            w_inter = jnp.exp(inter - m_t)
            s = lax.dot_general(q, k, (((1,), (1,)), ((), ())), preferred_element_type=F32) * w_intra
            cmat = c_ref[d]
            num = w_inter * jnp.dot(q, cmat.astype(BF16), preferred_element_type=F32) \
                + jnp.dot(s.astype(BF16), v, preferred_element_type=F32)
            qn = jnp.sum(q.astype(F32) * n_ref[d], axis=1, keepdims=True)
            den = w_inter * qn + jnp.sum(s, axis=1, keepdims=True)
            h = num / jnp.maximum(jnp.abs(den), jnp.exp(-m_t))

            b_last = b_r[:, last:last + 1]
            gl = b_last - b_c + li_c
            m_new = jnp.maximum(b_last + m_prev, jnp.max(gl, axis=0, keepdims=True))
            dec = jnp.exp(b_last + m_prev - m_new)
            kw = k.astype(F32) * jnp.exp(gl - m_new)
            c_ref[d] = dec * cmat + lax.dot_general(
                kw.astype(BF16), v, (((0,), (0,)), ((), ())), preferred_element_type=F32)
            n_ref[d] = dec * n_ref[d] + jnp.sum(kw, axis=0, keepdims=True)
            m_ref[d] = m_new

            if not partner_done:
                h_ref[pl.ds(r0, L), :] = h
            else:
                hs = h_ref[pl.ds(r0, L), :] + h
                hn = hs * lax.rsqrt(jnp.mean(hs * hs, axis=1, keepdims=True) + EPS)
                gate = jax.nn.sigmoid(o_ref[0, pl.ds(r0, L), :].astype(F32))
                out_ref[0, pl.ds(r0, L), :] = (hn * ng_ref[...] * gate).astype(out_ref.dtype)

        return chunk

    fwd, bwd = direction(False), direction(True)

    def step(c, partner_done):
        fwd(c, partner_done)
        bwd(nc - 1 - c, partner_done)

    def first_half(c, carry):
        step(c, False)
        return carry

    def second_half(c, carry):
        step(c, True)
        return carry

    lax.fori_loop(0, nc // 2, first_half, 0)
    lax.fori_loop(nc // 2, nc, second_half, 0)


def _mlstm(qk, u3, v_off, o_off, grow, gcol, norm_g):
    B, S, _ = qk.shape
    H, dk, dv = M_HEADS, M_DQK, M_DV
    L = _tile(S, M_CHUNK)
    vo, oo = v_off // dv, o_off // dv
    assert v_off % dv == 0 and o_off % dv == 0
    assert (S // L) % 2 == 0, "the two directions are paired chunk by chunk"
    return pl.pallas_call(
        functools.partial(_mlstm_body, L=L),
        grid=(B, H),
        in_specs=[pl.BlockSpec((1, S, dk), lambda b, h: (b, 0, h)),
                  pl.BlockSpec((1, S, dk), lambda b, h: (b, 0, H + h)),
                  pl.BlockSpec((1, S, dv), lambda b, h: (b, 0, vo + h)),
                  pl.BlockSpec((1, S, dv), lambda b, h: (b, 0, oo + h)),
                  pl.BlockSpec((1, 1, 8, S), lambda b, h: (b, h, 0, 0)),
                  pl.BlockSpec((1, 1, S, 8), lambda b, h: (b, h, 0, 0)),
                  pl.BlockSpec((1, dv), lambda b, h: (0, h))],
        out_specs=pl.BlockSpec((1, S, dv), lambda b, h: (b, 0, h)),
        out_shape=jax.ShapeDtypeStruct((B, S, H * dv), BF16),
        scratch_shapes=[pltpu.VMEM((2, dk, dv), F32), pltpu.VMEM((2, 1, dk), F32),
                        pltpu.VMEM((2, 1, 1), F32), pltpu.VMEM((S, dv), F32)],
        compiler_params=_cparams("parallel", "parallel"),
        name="mlstm",
    )(qk, qk, u3, u3, grow, gcol, norm_g.reshape(1, H * dv).astype(F32))


def _hy_mlp_body(z_ref, w1_ref, b1_ref, w2_ref, b2_ref, fr_ref, o_ref):
    fr = fr_ref[...]
    h = jnp.sin(fr * (jnp.dot(z_ref[...], w1_ref[...], precision=HI, preferred_element_type=F32) + b1_ref[...]))
    h = jnp.sin(fr * (jnp.dot(h, w2_ref[...], precision=HI, preferred_element_type=F32) + b2_ref[...]))
    o_ref[...] = h


def _hy_filter_body(h_ref, wf_ref, wb_ref, bf_ref, bb_ref, df_ref, db_ref, hs_ref, hd_ref, scr_ref):
    L = h_ref.shape[0]
    h2 = h_ref[...]
    tc = wf_ref.shape[1]
    row = lax.broadcasted_iota(jnp.int32, (L, tc), 0)
    tn = row.astype(F32) / L
    h2 = h2.astype(BF16)
    hf = (jnp.dot(h2, wf_ref[...].astype(BF16), preferred_element_type=F32) + bf_ref[...]) \
        * jnp.exp(-tn * jnp.abs(df_ref[...]))
    hb = (jnp.dot(h2, wb_ref[...].astype(BF16), preferred_element_type=F32) + bb_ref[...]) \
        * jnp.exp(-tn * jnp.abs(db_ref[...]))
    hb = jnp.where(row == 0, 0.0, hb)
    norm = jnp.sum(jnp.abs(hf), axis=0, keepdims=True) + jnp.sum(jnp.abs(hb), axis=0, keepdims=True)
    inv = 1.0 / norm
    for o_ref, val in ((hs_ref, (hf + hb) * inv), (hd_ref, (hb - hf) * inv)):
        def store(cols, even, odd, o_ref=o_ref):
            o_ref[0, :, cols] = even.astype(o_ref.dtype)
            o_ref[1, :, cols] = odd.astype(o_ref.dtype)
        _split_rows(scr_ref, val, store)


def _hy_filters(L, w1, b1, w2, b2, w3, b3, freq, decay, tc=256):
    hid = w1.shape[1]
    C = w3.shape[1] // (2 * H_ORDER)
    tn = jnp.arange(L, dtype=F32) / L
    bands = jnp.linspace(1e-4, H_BANDS - 1, H_BANDS, dtype=F32)
    ang = (2.0 * math.pi) * tn[:, None] * bands[None, :]
    z = jnp.concatenate([tn[:, None], jnp.cos(ang), jnp.sin(ang)], axis=-1)
    emb = z.shape[1]
    z = jnp.pad(z, ((0, 0), (0, LANES - emb)))
    w1p = jnp.pad(w1.astype(F32), ((0, LANES - emb), (0, 0)))
    row = lambda a: a.reshape(1, -1).astype(F32)
    h2 = pl.pallas_call(
        _hy_mlp_body,
        out_shape=jax.ShapeDtypeStruct((L, hid), F32),
        compiler_params=pltpu.CompilerParams(vmem_limit_bytes=VMEM_LIMIT),
        name="hyena_mlp",
    )(z, w1p, row(b1), w2.astype(F32), row(b2), row(freq))
    tc = _tile(C, tc)
    nct = C // tc
    fwd = lambda o, j: (0, o * nct + j)
    bwd = lambda o, j: (0, (H_ORDER + o) * nct + j)
    out = pl.BlockSpec((2, L // 2, tc), lambda o, j: (0, 0, o * nct + j))
    w3 = w3.astype(F32)
    b3r, dcr = row(b3), row(decay)
    return pl.pallas_call(
        _hy_filter_body,
        grid=(H_ORDER, nct),
        in_specs=[pl.BlockSpec((L, hid), lambda o, j: (0, 0)),
                  pl.BlockSpec((hid, tc), fwd), pl.BlockSpec((hid, tc), bwd),
                  pl.BlockSpec((1, tc), fwd), pl.BlockSpec((1, tc), bwd),
                  pl.BlockSpec((1, tc), fwd), pl.BlockSpec((1, tc), bwd)],
        out_specs=[out, out],
        out_shape=[jax.ShapeDtypeStruct((2, L // 2, H_ORDER * C), BF16)] * 2,
        scratch_shapes=[pltpu.VMEM((tc // LANES, L, LANES), F32)],
        compiler_params=_cparams("parallel", "parallel"),
        name="hyena_filter",
    )(h2, w3, w3, b3r, b3r, dcr, dcr)


def _dft_tables(L):
    L2 = L // 2
    kap = jnp.arange(L2, dtype=jnp.int32)
    m = ((2 * kap[:, None] + 1) * kap[None, :]) % (2 * L)
    ang = m.astype(F32) * F32(math.pi / L)
    gc, gs = jnp.cos(ang), jnp.sin(ang)
    w_lo = (2 * kap + 1).astype(F32) * F32(math.pi / (2 * L))
    w_hi = (2 * (L - 1 - kap) + 1).astype(F32) * F32(math.pi / (2 * L))
    tw = jnp.stack([jnp.cos(w_lo), jnp.sin(w_lo), jnp.cos(w_hi), jnp.sin(w_hi)], axis=1)
    return gc.astype(BF16), gs.astype(BF16), gc.T.astype(BF16), gs.T.astype(BF16), tw


def _twiddles(tw_ref):
    tw = tw_ref[...]
    return tw[:, 0:1], tw[:, 1:2], tw[:, 2:3], tw[:, 3:4]


def _hy_kspec_body(gc_ref, gs_ref, ae_ref, ao_ref, be_ref, bo_ref, tw_ref, kre_ref, kim_ref):
    gc, gs = gc_ref[...], gs_ref[...]
    c, s, ch, sh = _twiddles(tw_ref)
    dot = functools.partial(jnp.dot, preferred_element_type=F32)
    p1, p2, p3 = dot(gc, ae_ref[...]), dot(gc, ao_ref[...]), dot(gs, ao_ref[...])
    p4, p5, p6 = dot(gs, be_ref[...]), dot(gc, bo_ref[...]), dot(gs, bo_ref[...])
    kre_ref[0] = p1 + c * p2 - s * p3
    kim_ref[0] = p4 + s * p5 + c * p6
    kre_ref[1] = p1 + ch * p2 + sh * p3
    kim_ref[1] = -p4 + sh * p5 - ch * p6


def _hy_kspec(gc, gs, tw, hsum, hdiff, tm=512, tn=256):
    _, L2, OC = hsum.shape
    a2, b2 = hsum, hdiff
    tm, tn = _tile(L2, tm), _tile(OC, tn)
    nct = OC // tn
    tab = pl.BlockSpec((tm, L2), lambda j, i: (i, 0))
    even = pl.BlockSpec((None, L2, tn), lambda j, i: (0, 0, j))
    odd = pl.BlockSpec((None, L2, tn), lambda j, i: (1, 0, j))
    out = pl.BlockSpec((2, tm, tn), lambda j, i: (0, i, j))
    shp = jax.ShapeDtypeStruct((2, L2, OC), F32)
    return pl.pallas_call(
        _hy_kspec_body,
        grid=(nct, L2 // tm),
        in_specs=[tab, tab, even, odd, even, odd, pl.BlockSpec((tm, 4), lambda j, i: (i, 0))],
        out_specs=[out, out], out_shape=[shp, shp],
        compiler_params=_cparams("parallel", "parallel"),
        name="hyena_kspec",
    )(gc, gs, a2, a2, b2, b2, tw)


def _hy_fwd_body(gc_ref, gs_ref, ze_ref, zo_ref, kre_ref, kim_ref, tw_ref, wre_ref, wim_ref):
    gc, gs = gc_ref[...], gs_ref[...]
    c, s, ch, sh = _twiddles(tw_ref)
    kre_l, kim_l, kre_h, kim_h = kre_ref[0], kim_ref[0], kre_ref[1], kim_ref[1]
    dot = functools.partial(jnp.dot, preferred_element_type=F32)
    for b in range(ze_ref.shape[0]):
        r0, s0 = dot(gc, ze_ref[b]), dot(gs, ze_ref[b])
        r1, s1 = dot(gc, zo_ref[b]), dot(gs, zo_ref[b])
        zre_l = r0 + c * r1 - s * s1
        zim_l = -s0 - c * s1 - s * r1
        zre_h = r0 + ch * r1 + sh * s1
        zim_h = s0 + ch * s1 - sh * r1
        a = zre_l * kre_l - zim_l * kim_l
        bb = zre_l * kim_l + zim_l * kre_l
        p = zre_h * kre_h - zim_h * kim_h
        q = zre_h * kim_h + zim_h * kre_h
        wre_ref[b, 0] = (a + p).astype(wre_ref.dtype)
        wim_ref[b, 0] = (bb - q).astype(wim_ref.dtype)
        wre_ref[b, 1] = (a * c - bb * s + p * ch - q * sh).astype(wre_ref.dtype)
        wim_ref[b, 1] = (a * s + bb * c - p * sh - q * ch).astype(wim_ref.dtype)


def _hy_fwd(gc, gs, tw, z4, z_off, kre, kim, order, C, tm=512, tn=256):
    B, _, L2, _ = z4.shape
    tm, tn = _tile(L2, tm), _tile(C, tn)
    nct = C // tn
    zo = z_off // tn
    assert z_off % tn == 0
    tab = pl.BlockSpec((tm, L2), lambda j, i: (i, 0))
    ksp = pl.BlockSpec((2, tm, tn), lambda j, i: (0, i, order * nct + j))
    out = pl.BlockSpec((B, 2, tm, tn), lambda j, i: (0, 0, i, j))
    shp = jax.ShapeDtypeStruct((B, 2, L2, C), BF16)
    return pl.pallas_call(
        _hy_fwd_body,
        grid=(nct, L2 // tm),
        in_specs=[tab, tab,
                  pl.BlockSpec((B, None, L2, tn), lambda j, i: (0, 0, 0, zo + j)),
                  pl.BlockSpec((B, None, L2, tn), lambda j, i: (0, 1, 0, zo + j)),
                  ksp, ksp, pl.BlockSpec((tm, 4), lambda j, i: (i, 0))],
        out_specs=[out, out], out_shape=[shp, shp],
        compiler_params=_cparams("parallel", "parallel"),
        name="hyena_fwd",
    )(gc, gs, z4, z4, kre, kim, tw)


def _hy_inv_body(gct_ref, gst_ref, wre_ref, wim_ref, z_ref, g_ref, skip_ref, o_ref, *scratch, scale, interleave):
    gct, gst = gct_ref[...], gst_ref[...]
    skip = skip_ref[...]
    tm = gct.shape[0]
    dot = functools.partial(jnp.dot, preferred_element_type=F32)
    for b in range(z_ref.shape[0]):
        for j in range(2):
            y = dot(gct, wre_ref[b, j]) - dot(gst, wim_ref[b, j])
            out = g_ref[b, j].astype(F32) * (scale * y + skip * z_ref[b, j].astype(F32))
            if interleave:
                for g in range(out.shape[1] // LANES):
                    scratch[0][g, pl.ds(j, tm, stride=2), :] = out[:, g * LANES:(g + 1) * LANES]
            else:
                o_ref[b, j] = out.astype(o_ref.dtype)
        if interleave:
            for g in range(o_ref.shape[2] // LANES):
                o_ref[b, :, g * LANES:(g + 1) * LANES] = scratch[0][g].astype(o_ref.dtype)


def _hy_inv(gct, gst, wre, wim, z4, z_off, g4, g_off, skip, interleave, tm=512, tn=256):
    B, _, L2, C = wre.shape
    tm, tn = _tile(L2, tm), _tile(C, tn)
    zo, go = z_off // tn, g_off // tn
    assert z_off % tn == 0 and g_off % tn == 0
    tab = pl.BlockSpec((tm, L2), lambda j, i: (i, 0))
    wsp = pl.BlockSpec((B, 2, L2, tn), lambda j, i: (0, 0, 0, j))
    rows = lambda off: pl.BlockSpec((B, 2, tm, tn), lambda j, i: (0, 0, i, off + j))
    if interleave:
        out_spec = pl.BlockSpec((B, 2 * tm, tn), lambda j, i: (0, i, j))
        out_shape = jax.ShapeDtypeStruct((B, 2 * L2, C), BF16)
    else:
        out_spec, out_shape = rows(0), jax.ShapeDtypeStruct((B, 2, L2, C), BF16)
    return pl.pallas_call(
        functools.partial(_hy_inv_body, scale=1.0 / (2 * L2), interleave=interleave),
        grid=(C // tn, L2 // tm),
        in_specs=[tab, tab, wsp, wsp, rows(zo), rows(go), pl.BlockSpec((1, tn), lambda j, i: (0, j))],
        out_specs=out_spec, out_shape=out_shape,
        scratch_shapes=[pltpu.VMEM((tn // LANES, 2 * tm, LANES), F32)] if interleave else [],
        compiler_params=_cparams("parallel", "parallel"),
        name="hyena_inv",
    )(gct, gst, wre, wim, z4, g4, skip.reshape(1, C).astype(F32))


def _top_values(x, n):
    rank = lax.broadcasted_iota(jnp.int32, (n, x.shape[1]), 0)
    vals = jnp.zeros((n, x.shape[1]), F32)
    work = x
    for r in range(n):
        mx = jnp.max(work, axis=0, keepdims=True)
        vals = jnp.where(rank == r, mx, vals)
        work = jnp.where(work >= mx, -jnp.inf, work)
    return vals, mx


def _peer_route_body(q_ref, k_ref, s1_ref, e1_ref, th_ref, e0_ref):
    nt = (((1,), (1,)), ((), ()))
    s0 = lax.dot_general(k_ref[0], q_ref[0], nt, preferred_element_type=F32)
    s1 = lax.dot_general(k_ref[1], q_ref[1], nt, preferred_element_type=F32)
    v0, _ = _top_values(s0, P_TOPK)
    v1, _ = _top_values(s1, P_TOPK)
    k, oct_ = P_TOPK, 8
    assert k == 2 * oct_
    pieces = [v0[0:1, :] + v1]
    pieces += [v0[a:a + 1, :] + v1[0:oct_, :] for a in range(1, oct_)]
    pieces += [v0[oct_:k, :] + v1[0:1, :]]
    cand = jnp.concatenate(pieces, axis=0)
    _, tau = _top_values(cand, P_TOPK)
    top = v0[0:1, :] + v1[0:1, :]
    zsum = jnp.sum(jnp.where(cand >= tau, jnp.exp(cand - top), 0.0), axis=0, keepdims=True)
    th = jnp.full(s0.shape, jnp.inf, F32)
    for b in range(P_TOPK):
        vb = v1[b:b + 1, :]
        th = jnp.where(s0 + vb >= tau, vb, th)
    s1_ref[0] = s1
    e1_ref[0] = jnp.exp(s1 - v1[0:1, :])
    th_ref[0] = th
    e0_ref[0] = jnp.exp(s0 - v0[0:1, :]) / zsum


def _peer_route(q3, keys, tm=256):
    H2, T, dk = q3.shape
    H = H2 // 2
    tm = _tile(T, tm)
    big = pl.BlockSpec((1, N_KEYS, tm), lambda i, h: (h, 0, i))
    shp = jax.ShapeDtypeStruct((H, N_KEYS, T), F32)
    return pl.pallas_call(
        _peer_route_body,
        grid=(T // tm, H),
        in_specs=[pl.BlockSpec((2, tm, dk), lambda i, h: (h, i, 0)),
                  pl.BlockSpec((2, N_KEYS, dk), lambda i, h: (h, 0, 0))],
        out_specs=[big] * 4, out_shape=[shp] * 4,
        compiler_params=_cparams("parallel", "parallel"),
        name="peer_route",
    )(q3, keys)


GATE_ROWS = 32


def _peer_expert_body(xt_ref, u_ref, vt_ref, s1_ref, e1_ref, th_ref, e0_ref, o_ref, a_ref, w_ref, *, te):
    e = pl.program_id(1)
    tm = a_ref.shape[1]

    @pl.when(e == 0)
    def _():
        o_ref[...] = jnp.zeros_like(o_ref)

    half = tm // 2 if tm % (2 * LANES) == 0 else tm
    halves = [slice(s, s + half) for s in range(0, tm, half)]

    rows_i = []
    for ii in range(te // N_KEYS):
        i_idx = e * (te // N_KEYS) + ii
        rows_i.append(([th_ref[h, pl.ds(i_idx, 1), :] for h in range(P_HEADS)],
                       [e0_ref[h, pl.ds(i_idx, 1), :] for h in range(P_HEADS)]))

    def scores(hs):
        a_ref[:, hs] = jnp.dot(u_ref[...], xt_ref[:, hs], preferred_element_type=F32)

    def weights(hs):
        for ii in range(te // N_KEYS):
            th_full, e0_full = rows_i[ii]
            for tc in range(hs.start // LANES, hs.stop // LANES):
                cols = slice(tc * LANES, (tc + 1) * LANES)
                th_rows = [r[:, cols] for r in th_full]
                e0_rows = [r[:, cols] for r in e0_full]
                for jb in range(N_KEYS // GATE_ROWS):
                    rows = slice(jb * GATE_ROWS, (jb + 1) * GATE_ROWS)
                    g = None
                    for h in range(P_HEADS):
                        term = e0_rows[h] * jnp.where(s1_ref[h, rows, cols] >= th_rows[h],
                                                      e1_ref[h, rows, cols], 0.0)
                        g = term if g is None else g + term
                    erows = slice(ii * N_KEYS + jb * GATE_ROWS, ii * N_KEYS + (jb + 1) * GATE_ROWS)
                    a = a_ref[erows, cols]
                    act = 0.5 * a * (1.0 + lax.erf(a * (1.0 / math.sqrt(2.0))))
                    w_ref[erows, cols] = (g * act).astype(BF16)

    def project(hs):
        o_ref[:, hs] += jnp.dot(vt_ref[...], w_ref[:, hs], preferred_element_type=F32)

    for hs in halves:
        scores(hs)
        weights(hs)
    for hs in halves:
        project(hs)


def _peer_expert(xt, U, Vt, layer, s1, e1, th, e0, tm=512, te=512):
    D, T = xt.shape
    E = U.shape[1]
    tm, te = _tile(T, tm), _tile(E, te)
    H = s1.shape[0]
    once = dict(pipeline_mode=pl.Buffered(1))
    rt = pl.BlockSpec((H, N_KEYS, tm), lambda i, e: (0, 0, i), **once)
    return pl.pallas_call(
        functools.partial(_peer_expert_body, te=te),
        grid=(T // tm, E // te),
        in_specs=[pl.BlockSpec((D, tm), lambda i, e: (0, i), **once),
                  pl.BlockSpec((None, te, D), lambda i, e: (layer, e, 0)),
                  pl.BlockSpec((None, D, te), lambda i, e: (layer, 0, e)),
                  rt, rt, rt, rt],
        out_specs=pl.BlockSpec((D, tm), lambda i, e: (0, i)),
        out_shape=jax.ShapeDtypeStruct((D, T), F32),
        scratch_shapes=[pltpu.VMEM((te, tm), F32), pltpu.VMEM((te, tm), BF16)],
        compiler_params=_cparams("parallel", "arbitrary"),
        name="peer_expert",
    )(xt, U, Vt, s1, e1, th, e0)


def _layer(l, x2, xn, B, S, P, tables):
    T, D = x2.shape
    H, dk, dv = M_HEADS, M_DQK, M_DV
    mqk, mv = H * dk, H * dv
    n_gate = 4 * H
    C = P["hy_skip"].shape[2]
    m_cols = 2 * mqk + 2 * mv + n_gate
    assert (m_cols - n_gate) % LANES == 0
    w_in = P["w_in"][l]

    u_m = _matmul(xn, w_in[:, :m_cols - n_gate].astype(BF16), BF16, name="in_proj_mlstm")
    w_gate = jnp.pad(w_in[:, m_cols - n_gate:m_cols], ((0, 0), (0, LANES - n_gate))).astype(BF16)
    b_gate = jnp.pad(P["m_gate_b"][l], (0, LANES - n_gate))
    gates = _matmul(xn, w_gate, F32, bias=b_gate, name="gate_proj")
    u_hg = _matmul(xn, w_in[:, m_cols:].astype(BF16), BF16, name="in_proj_hyena")
    um3 = u_m.reshape(B, S, u_m.shape[1])

    qk_scale = jnp.concatenate([jnp.ones((mqk,), F32), jnp.full((mqk,), dk ** -0.5, F32)])
    qk = _dwconv(um3, 0, P["m_conv"][l], qk_scale, silu=True)
    g5 = gates[:, :n_gate].reshape(B, S, 4, H)
    grow = jnp.pad(jnp.transpose(g5, (0, 3, 2, 1)), ((0, 0), (0, 0), (0, 4), (0, 0)))
    gcol = jnp.pad(jnp.transpose(g5, (0, 3, 1, 2)), ((0, 0), (0, 0), (0, 0), (0, 4)))
    hm = _mlstm(qk, um3, 2 * mqk, 2 * mqk + mv, grow, gcol, P["m_norm_g"][l])

    gc, gs, gct, gst, tw = tables
    nh = (H_ORDER + 1) * C
    hy4 = _dwconv(u_hg.reshape(B, S, u_hg.shape[1]), 0, P["hy_conv"][l], jnp.ones((nh,), F32), silu=False,
                  split=True)
    hsum, hdiff = _hy_filters(S, P["hy_w1"][l], P["hy_b1"][l], P["hy_w2"][l], P["hy_b2"][l], P["hy_w3"][l],
                              P["hy_b3"][l], P["hy_freq"][l], P["hy_decay"][l])
    kre, kim = _hy_kspec(gc, gs, tw, hsum, hdiff)
    z4, z_off = hy4, 0
    for o in range(H_ORDER):
        wre, wim = _hy_fwd(gc, gs, tw, z4, z_off, kre, kim, o, C)
        z4 = _hy_inv(gct, gst, wre, wim, z4, z_off, hy4, (o + 1) * C, P["hy_skip"][l, o],
                     interleave=o == H_ORDER - 1)
        z_off = 0
    hy = z4.reshape(T, C)

    merged = _merge_matmul(hm.reshape(T, mv), P["w_mo_bf16"], hy, P["w_ho_bf16"], l, u_hg, nh, nh + D)
    x2 = _matmul(merged, P["w_o"], F32, layer=l, residual=x2, tn=512, name="out_proj")

    xn2, xn2t = _rmsnorm(x2, P["ln2_g"][l], BF16, want_t=True)
    q = _matmul(xn2, P["peer_wq"], BF16, layer=l, tn=512, name="peer_query")
    q3 = jnp.transpose(q.reshape(T, 2 * P_HEADS, P_DKEY), (1, 0, 2))
    keys = P["peer_keys"][l].reshape(2 * P_HEADS, N_KEYS, P_DKEY).astype(BF16)
    s1, e1, th, e0 = _peer_route(q3, keys)
    yt = _peer_expert(xn2t, P["peer_u_bf16"], P["peer_vt_bf16"], l, s1, e1, th, e0)
    return x2, yt


_LAYER_PARAMS = ("ln1_g", "ln2_g", "w_in", "m_conv", "m_gate_b", "m_norm_g", "hy_conv", "hy_w1", "hy_b1",
                 "hy_w2", "hy_b2", "hy_w3", "hy_b3", "hy_freq", "hy_decay", "hy_skip", "w_mo", "w_ho", "w_o",
                 "peer_wq", "peer_keys", "peer_u", "peer_v")


def kernel(x, ln1_g, ln2_g, w_in, m_conv, m_gate_b, m_norm_g, hy_conv, hy_w1, hy_b1, hy_w2, hy_b2, hy_w3, hy_b3,
           hy_freq, hy_decay, hy_skip, w_mo, w_ho, w_o, peer_wq, peer_keys, peer_u, peer_v, final_g):
    P = dict(zip(_LAYER_PARAMS, (ln1_g, ln2_g, w_in, m_conv, m_gate_b, m_norm_g, hy_conv, hy_w1, hy_b1,
                                 hy_w2, hy_b2, hy_w3, hy_b3, hy_freq, hy_decay, hy_skip, w_mo, w_ho, w_o,
                                 peer_wq, peer_keys, peer_u, peer_v)))
    P["w_mo_bf16"], P["w_ho_bf16"] = w_mo.astype(BF16), w_ho.astype(BF16)
    P["peer_u_bf16"] = peer_u.astype(BF16)
    P["peer_vt_bf16"] = jnp.swapaxes(peer_v, 1, 2).astype(BF16)
    B, S, D = x.shape
    depth = ln1_g.shape[0]
    tables = _dft_tables(S)
    x2 = x.reshape(B * S, D)
    (xn,) = _rmsnorm(x2, ln1_g[0], BF16)
    for l in range(depth):
        x2, yt = _layer(l, x2, xn, B, S, P, tables)
        if l + 1 < depth:
            x2, xn = _rmsnorm(x2, ln1_g[l + 1], BF16, add_t=yt, want_sum=True)
        else:
            (out,) = _rmsnorm(x2, final_g, F32, add_t=yt)
    return out.reshape(B, S, D)
```
